```python
import math
import jax, jax.numpy as jnp
from jax import lax
import numpy as np

D_MODEL = 1024
BATCH = 4
SEQ = 8192
DEPTH = 2

CHUNK = 64
N_MIXERS = 2
N_ATTN_LAYERS = (DEPTH + 1) // 2
N_CONV_LAYERS = DEPTH // 2
N_HEADS = 8
HEAD_DIM = 64
V_DIM = 2 * HEAD_DIM
ROPE_THETA = 10000.0
Q_BLOCK = 128
CONV_WIDTH = 31
D_FF = -(-(8 * D_MODEL) // (3 * 256)) * 256
LN_EPS = 1e-5
DEEPNORM_ALPHA = (2.0 * DEPTH) ** 0.25
DEEPNORM_BETA = (8.0 * DEPTH) ** -0.25
MASK_VALUE = -1e30

kernel_name = "hybrid_diffattn_conformer_conv_deepnorm"


def layer_norm(x, g, b):
    xf = x.astype(jnp.float32)
    mu = jnp.mean(xf, axis=-1, keepdims=True)
    var = jnp.mean(jnp.square(xf - mu), axis=-1, keepdims=True)
    y = (xf - mu) * lax.rsqrt(var + LN_EPS)
    return (y * g.astype(jnp.float32) + b.astype(jnp.float32)).astype(x.dtype)


def rms_norm(x, g):
    xf = x.astype(jnp.float32)
    y = xf * lax.rsqrt(jnp.mean(jnp.square(xf), axis=-1, keepdims=True) + LN_EPS)
    return (y * g.astype(jnp.float32)).astype(x.dtype)


def rope_tables(seq_len, dim):
    pos = jnp.arange(seq_len, dtype=jnp.float32)
    inv_freq = ROPE_THETA ** (-jnp.arange(0, dim, 2, dtype=jnp.float32) / dim)
    ang = pos[:, None] * inv_freq[None, :]
    return jnp.cos(ang), jnp.sin(ang)


def apply_rope(x, cos, sin):
    half = x.shape[-1] // 2
    x1, x2 = x[..., :half], x[..., half:]
    c = cos[None, :, None, None, :].astype(x.dtype)
    s = sin[None, :, None, None, :].astype(x.dtype)
    return jnp.concatenate([x1 * c - x2 * s, x2 * c + x1 * s], axis=-1)


def diff_attention(x, w_qkv, w_o, lq1, lk1, lq2, lk2, subln_g, lambda_init):
    B, S, D = x.shape
    n_blocks = S // Q_BLOCK
    qkv = x @ w_qkv
    q, k, v = jnp.split(qkv, 3, axis=-1)
    q = q.reshape(B, S, N_HEADS, 2, HEAD_DIM)
    k = k.reshape(B, S, N_HEADS, 2, HEAD_DIM)
    v = v.reshape(B, S, N_HEADS, V_DIM).transpose(0, 2, 1, 3)
    cos, sin = rope_tables(S, HEAD_DIM)
    q = apply_rope(q, cos, sin) * (HEAD_DIM ** -0.5)
    k = apply_rope(k, cos, sin)
    q = q.transpose(0, 2, 3, 1, 4)
    k = k.transpose(0, 2, 3, 1, 4)
    q_blocks = jnp.moveaxis(q.reshape(B, N_HEADS, 2, n_blocks, Q_BLOCK, HEAD_DIM), 3, 0)

    lam = (jnp.exp(jnp.sum(lq1.astype(jnp.float32) * lk1.astype(jnp.float32)))
           - jnp.exp(jnp.sum(lq2.astype(jnp.float32) * lk2.astype(jnp.float32)))
           + lambda_init)
    key_chunk = jnp.arange(S) // CHUNK

    def block_fn(args):
        qb, blk = args
        s = jnp.einsum('bhmqd,bhmkd->bhmqk', qb, k).astype(jnp.float32)
        q_chunk = (blk * Q_BLOCK + jnp.arange(Q_BLOCK)) // CHUNK
        mask = key_chunk[None, :] <= q_chunk[:, None]
        s = jnp.where(mask[None, None, None], s, MASK_VALUE)
        p = jax.nn.softmax(s, axis=-1)
        attn = p[:, :, 0] - lam * p[:, :, 1]
        return jnp.einsum('bhqk,bhkv->bhqv', attn.astype(v.dtype), v)

    out = lax.map(block_fn, (q_blocks, jnp.arange(n_blocks)))
    out = out.transpose(1, 0, 3, 2, 4).reshape(B, S, N_HEADS, V_DIM)
    out = rms_norm(out, subln_g) * (1.0 - lambda_init)
    return out.reshape(B, S, N_HEADS * V_DIM) @ w_o


def conformer_conv(x, w_pw1, b_pw1, w_dw, b_dw, ln_g, ln_b, w_pw2, b_pw2):
    D = x.shape[-1]
    h = x @ w_pw1 + b_pw1
    a, gate = jnp.split(h, 2, axis=-1)
    h = a * jax.nn.sigmoid(gate)
    h = jnp.pad(h, ((0, 0), (CONV_WIDTH - 1, 0), (0, 0)))
    h = lax.conv_general_dilated(
        h, w_dw[:, None, :].astype(h.dtype), window_strides=(1,), padding='VALID',
        dimension_numbers=('NWC', 'WIO', 'NWC'), feature_group_count=D) + b_dw
    h = layer_norm(h, ln_g, ln_b)
    h = jax.nn.silu(h)
    return h @ w_pw2 + b_pw2


def swiglu_ffn(x, w_gate, w_up, w_down):
    return (jax.nn.silu(x @ w_gate) * (x @ w_up)) @ w_down


def setup_inputs(seed: int = 0) -> dict:
    key = jax.random.key(seed)
    ks = jax.random.split(key, 24)
    D = D_MODEL
    nrm = lambda k, shape, scale: jax.random.normal(k, shape, jnp.float32) * scale
    return {
        "x": nrm(ks[0], (BATCH, SEQ, D), 1.0),
        "attn_w_qkv": nrm(ks[1], (N_ATTN_LAYERS, D, 3 * D), D ** -0.5),
        "attn_w_o": nrm(ks[2], (N_ATTN_LAYERS, N_HEADS * V_DIM, D), DEEPNORM_BETA * D ** -0.5),
        "attn_lambda_q1": nrm(ks[3], (N_ATTN_LAYERS, HEAD_DIM), 0.1),
        "attn_lambda_k1": nrm(ks[4], (N_ATTN_LAYERS, HEAD_DIM), 0.1),
        "attn_lambda_q2": nrm(ks[5], (N_ATTN_LAYERS, HEAD_DIM), 0.1),
        "attn_lambda_k2": nrm(ks[6], (N_ATTN_LAYERS, HEAD_DIM), 0.1),
        "attn_subln_g": 1.0 + nrm(ks[7], (N_ATTN_LAYERS, V_DIM), 0.02),
        "conv_w_pw1": nrm(ks[8], (N_CONV_LAYERS, D, 2 * D), D ** -0.5),
        "conv_b_pw1": nrm(ks[9], (N_CONV_LAYERS, 2 * D), 0.02),
        "conv_w_dw": nrm(ks[10], (N_CONV_LAYERS, CONV_WIDTH, D), CONV_WIDTH ** -0.5),
        "conv_b_dw": nrm(ks[11], (N_CONV_LAYERS, D), 0.02),
        "conv_ln_g": 1.0 + nrm(ks[12], (N_CONV_LAYERS, D), 0.02),
        "conv_ln_b": nrm(ks[13], (N_CONV_LAYERS, D), 0.02),
        "conv_w_pw2": nrm(ks[14], (N_CONV_LAYERS, D, D), DEEPNORM_BETA * D ** -0.5),
        "conv_b_pw2": nrm(ks[15], (N_CONV_LAYERS, D), 0.02),
        "ffn_w_gate": nrm(ks[16], (DEPTH, D, D_FF), D ** -0.5),
        "ffn_w_up": nrm(ks[17], (DEPTH, D, D_FF), D ** -0.5),
        "ffn_w_down": nrm(ks[18], (DEPTH, D_FF, D), DEEPNORM_BETA * D_FF ** -0.5),
        "ln_g": 1.0 + nrm(ks[19], (DEPTH, 2, D), 0.02),
        "ln_b": nrm(ks[20], (DEPTH, 2, D), 0.02),
    }


def reference(x, attn_w_qkv, attn_w_o, attn_lambda_q1, attn_lambda_k1, attn_lambda_q2,
              attn_lambda_k2, attn_subln_g, conv_w_pw1, conv_b_pw1, conv_w_dw, conv_b_dw,
              conv_ln_g, conv_ln_b, conv_w_pw2, conv_b_pw2, ffn_w_gate, ffn_w_up,
              ffn_w_down, ln_g, ln_b):
    for i in range(DEPTH):
        j = i // N_MIXERS
        if i % N_MIXERS == 0:
            lambda_init = 0.8 - 0.6 * math.exp(-0.3 * i)
            y = diff_attention(x, attn_w_qkv[j], attn_w_o[j], attn_lambda_q1[j],
                               attn_lambda_k1[j], attn_lambda_q2[j], attn_lambda_k2[j],
                               attn_subln_g[j], lambda_init)
        else:
            y = conformer_conv(x, conv_w_pw1[j], conv_b_pw1[j], conv_w_dw[j], conv_b_dw[j],
                               conv_ln_g[j], conv_ln_b[j], conv_w_pw2[j], conv_b_pw2[j])
        x = layer_norm(DEEPNORM_ALPHA * x + y, ln_g[i, 0], ln_b[i, 0])
        f = swiglu_ffn(x, ffn_w_gate[i], ffn_w_up[i], ffn_w_down[i])
        x = layer_norm(DEEPNORM_ALPHA * x + f, ln_g[i, 1], ln_b[i, 1])
    return x
```

```python
import functools
import math

import jax
import jax.numpy as jnp
from jax import lax
from jax.experimental import pallas as pl
from jax.experimental.pallas import tpu as pltpu

N_HEADS = 8
HEAD_DIM = 64
V_DIM = 2 * HEAD_DIM
CHUNK = 64
ROPE_THETA = 10000.0
CONV_WIDTH = 31
LN_EPS = 1e-5
MASK_VALUE = -1e30
N_MIXERS = 2

V7X_VMEM_BYTES = 64 * 1024 * 1024
VMEM_LIMIT_BYTES = V7X_VMEM_BYTES - 8 * 1024 * 1024
SUBLANES = 8
LANES = 128

ROW_TILE = 512
ATTN_TILE = 256
HALO_ROWS = 32

BF16 = jnp.bfloat16
F32 = jnp.float32


def _params(n_axes):
    return pltpu.CompilerParams(
        dimension_semantics=("arbitrary",) * n_axes,
        vmem_limit_bytes=VMEM_LIMIT_BYTES,
    )


def _resident(shape):
    zeros = (0,) * len(shape)
    return pl.BlockSpec(shape, lambda *_: zeros, pipeline_mode=pl.Buffered(1))


def _layer_norm(z, g, b):
    mu = jnp.mean(z, axis=-1, keepdims=True)
    zc = z - mu
    var = jnp.mean(zc * zc, axis=-1, keepdims=True)
    return zc * lax.rsqrt(var + LN_EPS) * g + b


def _sigmoid(v):
    return 1.0 / (1.0 + jnp.exp(-v))


def _qkv_rope_kernel(x_ref, wqt_ref, wk_ref, wvt_ref, cos_ref, sin_lo_ref, sin_hi_ref,
                     cost_ref, sint_ref, qt_ref, k_ref, vt_ref):
    xb = x_ref[...].astype(BF16)
    nt = (((1,), (1,)), ((), ()))

    k = jnp.dot(xb, wk_ref[...], preferred_element_type=F32)
    cos = cos_ref[...]
    sin_lo = sin_lo_ref[...]
    sin_hi = sin_hi_ref[...]
    half = HEAD_DIM // 2
    for h in range(N_HEADS):
        kh = k[:, h * V_DIM:(h + 1) * V_DIM]
        rot = (kh * cos
               + pltpu.roll(kh, V_DIM - half, 1) * sin_lo
               + pltpu.roll(kh, half, 1) * sin_hi)
        k_ref[:, h * V_DIM:(h + 1) * V_DIM] = rot.astype(BF16)

    qt = lax.dot_general(wqt_ref[...], xb, nt, preferred_element_type=F32)
    cost = cost_ref[...]
    sint = sint_ref[...]
    scale = HEAD_DIM ** -0.5
    for g in range(2 * N_HEADS):
        r0 = g * HEAD_DIM
        x1 = qt[r0:r0 + half]
        x2 = qt[r0 + half:r0 + HEAD_DIM]
        qt_ref[r0:r0 + half, :] = ((x1 * cost - x2 * sint) * scale).astype(BF16)
        qt_ref[r0 + half:r0 + HEAD_DIM, :] = ((x2 * cost + x1 * sint) * scale).astype(BF16)

    vt = lax.dot_general(wvt_ref[...], xb, nt, preferred_element_type=F32)
    vt_ref[...] = vt.astype(BF16)


def _qkv_rope(x, w_qkv):
    B, S, D = x.shape
    tm = min(ROW_TILE, S)
    half = HEAD_DIM // 2
    wq, wk, wv = jnp.split(w_qkv, 3, axis=-1)
    wqt = wq.T.astype(BF16)
    wvt = wv.T.astype(BF16)
    wk = wk.astype(BF16)

    pos = jnp.arange(S, dtype=F32)
    inv_freq = ROPE_THETA ** (-jnp.arange(0, HEAD_DIM, 2, dtype=F32) / HEAD_DIM)
    ang = pos[:, None] * inv_freq[None, :]
    cos, sin = jnp.cos(ang), jnp.sin(ang)
    reps = V_DIM // half
    cos_l = jnp.tile(cos, (1, reps))
    sin_l = jnp.tile(sin, (1, reps))
    is_x1 = ((jnp.arange(V_DIM) // half) % 2 == 0)[None, :]
    sin_lo = jnp.where(is_x1, -sin_l, 0.0)
    sin_hi = jnp.where(is_x1, 0.0, sin_l)

    row = lambda b, i: (b, i, 0)
    col = lambda b, i: (b, 0, i)
    tab = pl.BlockSpec((tm, V_DIM), lambda b, i: (i, 0))
    tabt = pl.BlockSpec((half, tm), lambda b, i: (0, i))
    return pl.pallas_call(
        _qkv_rope_kernel,
        grid=(B, S // tm),
        in_specs=[pl.BlockSpec((None, tm, D), row),
                  _resident((D, D)), _resident((D, D)), _resident((D, D)),
                  tab, tab, tab, tabt, tabt],
        out_specs=[pl.BlockSpec((None, D, tm), col),
                   pl.BlockSpec((None, tm, D), row),
                   pl.BlockSpec((None, D, tm), col)],
        out_shape=[jax.ShapeDtypeStruct((B, D, S), BF16),
                   jax.ShapeDtypeStruct((B, S, D), BF16),
                   jax.ShapeDtypeStruct((B, D, S), BF16)],
        compiler_params=_params(2),
        name="qkv_rope",
    )(x, wqt, wk, wvt, cos_l, sin_lo, sin_hi, cos.T, sin.T)


def _diff_attn_kernel(qt_ref, k_ref, vt_ref, lam_ref, g_ref, o_ref,
                      qbd_ref, acc_ref, m_ref, l_ref, *, lambda_init):
    t = ATTN_TILE
    qi = pl.program_id(2)

    q = qt_ref[...]
    zero = jnp.zeros((HEAD_DIM, t), BF16)
    qbd_ref[:HEAD_DIM, :t] = q[:HEAD_DIM]
    qbd_ref[:HEAD_DIM, t:] = zero
    qbd_ref[HEAD_DIM:, :t] = zero
    qbd_ref[HEAD_DIM:, t:] = q[HEAD_DIM:]
    m_ref[...] = jnp.full(m_ref.shape, MASK_VALUE, F32)
    l_ref[...] = jnp.zeros(l_ref.shape, F32)
    acc_ref[...] = jnp.zeros(acc_ref.shape, F32)

    def step(ki, diagonal):
        k0 = pl.multiple_of(ki * t, t)
        s = jnp.dot(k_ref[pl.ds(k0, t), :], qbd_ref[...], preferred_element_type=F32)
        if diagonal:
            key_chunk = lax.broadcasted_iota(jnp.int32, s.shape, 0) // CHUNK
            q_chunk = (lax.broadcasted_iota(jnp.int32, s.shape, 1) % t) // CHUNK
            s = jnp.where(key_chunk <= q_chunk, s, MASK_VALUE)
        m_old = m_ref[...]
        m_new = jnp.maximum(m_old, jnp.max(s, axis=0, keepdims=True))
        alpha = jnp.exp(m_old - m_new)
        p = jnp.exp(s - m_new)
        l_ref[...] = alpha * l_ref[...] + jnp.sum(p, axis=0, keepdims=True)
        m_ref[...] = m_new
        pv = jnp.dot(vt_ref[:, pl.ds(k0, t)], p.astype(BF16), preferred_element_type=F32)
        acc_ref[...] = alpha * acc_ref[...] + pv

    def body(ki, carry):
        step(ki, False)
        return carry

    lax.fori_loop(0, qi, body, 0)
    step(qi, True)

    lq = lam_ref[...]
    lam = (jnp.exp(jnp.sum(lq[0:1] * lq[1:2], axis=1, keepdims=True))
           - jnp.exp(jnp.sum(lq[2:3] * lq[3:4], axis=1, keepdims=True))
           + lambda_init)
    o = acc_ref[...] / l_ref[...]
    d = o[:, :t] - lam * o[:, t:]
    ms = jnp.mean(d * d, axis=0, keepdims=True)
    y = d * lax.rsqrt(ms + LN_EPS) * g_ref[...] * (1.0 - lambda_init)
    o_ref[...] = y.T.astype(BF16)


def _diff_attn(qt, k, vt, lam_vecs, subln_g, lambda_init):
    B, D, S = qt.shape
    t = ATTN_TILE
    assert S % t == 0 and t % CHUNK == 0
    return pl.pallas_call(
        functools.partial(_diff_attn_kernel, lambda_init=lambda_init),
        grid=(B, N_HEADS, S // t),
        in_specs=[pl.BlockSpec((None, V_DIM, t), lambda b, h, i: (b, h, i)),
                  pl.BlockSpec((None, S, V_DIM), lambda b, h, i: (b, 0, h)),
                  pl.BlockSpec((None, V_DIM, S), lambda b, h, i: (b, h, 0)),
                  pl.BlockSpec((4, HEAD_DIM), lambda b, h, i: (0, 0)),
                  pl.BlockSpec((V_DIM, 1), lambda b, h, i: (0, 0))],
        out_specs=pl.BlockSpec((None, t, V_DIM), lambda b, h, i: (b, i, h)),
        out_shape=jax.ShapeDtypeStruct((B, S, D), BF16),
        scratch_shapes=[pltpu.VMEM((V_DIM, 2 * t), BF16),
                        pltpu.VMEM((V_DIM, 2 * t), F32),
                        pltpu.VMEM((1, 2 * t), F32),
                        pltpu.VMEM((1, 2 * t), F32)],
        compiler_params=_params(3),
        name="diff_attn",
    )(qt, k, vt, lam_vecs, subln_g.reshape(V_DIM, 1))


def _proj_ln_kernel(a_ref, w_ref, x_ref, g_ref, b_ref, o_ref, *, alpha):
    y = jnp.dot(a_ref[...], w_ref[...], preferred_element_type=F32)
    o_ref[...] = _layer_norm(alpha * x_ref[...] + y, g_ref[...], b_ref[...])


def _proj_ln(a, w, x, g, b, alpha):
    N, D = x.shape
    tm = min(ROW_TILE, N)
    row = pl.BlockSpec((tm, D), lambda i: (i, 0))
    vec = pl.BlockSpec((1, D), lambda i: (0, 0))
    return pl.pallas_call(
        functools.partial(_proj_ln_kernel, alpha=alpha),
        grid=(N // tm,),
        in_specs=[pl.BlockSpec((tm, a.shape[1]), lambda i: (i, 0)), _resident(w.shape), row, vec, vec],
        out_specs=row,
        out_shape=jax.ShapeDtypeStruct((N, D), F32),
        compiler_params=_params(1),
        name="proj_ln",
    )(a, w.astype(BF16), x, g.reshape(1, D), b.reshape(1, D))


def _ffn_ln_kernel(x_ref, wg_ref, wu_ref, wd_ref, g_ref, b_ref, o_ref, *, alpha):
    x = x_ref[...]
    xb = x.astype(BF16)
    gate = jnp.dot(xb, wg_ref[...], preferred_element_type=F32)
    up = jnp.dot(xb, wu_ref[...], preferred_element_type=F32)
    h = (gate * _sigmoid(gate)) * up
    f = jnp.dot(h.astype(BF16), wd_ref[...], preferred_element_type=F32)
    o_ref[...] = _layer_norm(alpha * x + f, g_ref[...], b_ref[...])


def _ffn_ln(x, wg, wu, wd, g, b, alpha):
    N, D = x.shape
    tm = min(ROW_TILE, N)
    row = pl.BlockSpec((tm, D), lambda i: (i, 0))
    vec = pl.BlockSpec((1, D), lambda i: (0, 0))
    return pl.pallas_call(
        functools.partial(_ffn_ln_kernel, alpha=alpha),
        grid=(N // tm,),
        in_specs=[row, _resident(wg.shape), _resident(wu.shape), _resident(wd.shape), vec, vec],
        out_specs=row,
        out_shape=jax.ShapeDtypeStruct((N, D), F32),
        compiler_params=_params(1),
        name="ffn_ln",
    )(x, wg.astype(BF16), wu.astype(BF16), wd.astype(BF16), g.reshape(1, D), b.reshape(1, D))


def _conv_block_kernel(x_ref, w1_ref, b1_ref, wdw_ref, bdw_ref, cg_ref, cb_ref, w2_ref, b2_ref,
                       g_ref, b_ref, o_ref, hist_ref, *, alpha):
    tm, D = x_ref.shape
    x = x_ref[...]
    h = jnp.dot(x.astype(BF16), w1_ref[...], preferred_element_type=F32) + b1_ref[...]
    glu = h[:, :D] * _sigmoid(h[:, D:])

    @pl.when(pl.program_id(1) == 0)
    def _():
        hist_ref[:HALO_ROWS, :] = jnp.zeros((HALO_ROWS, D), F32)

    hist_ref[HALO_ROWS:, :] = glu

    base = HALO_ROWS - (CONV_WIDTH - 1)
    wdw = wdw_ref[...]
    acc = jnp.zeros((tm, D), F32)
    for r in range(SUBLANES):
        taps = [j for j in range(r, CONV_WIDTH, SUBLANES)]
        span = tm + taps[-1] - r
        shifted = hist_ref[base + r:base + r + span, :]
        for j in taps:
            acc = acc + wdw[j:j + 1, :] * shifted[j - r:j - r + tm, :]
    conv = acc + bdw_ref[...]

    hist_ref[:HALO_ROWS, :] = hist_ref[tm:tm + HALO_ROWS, :]

    c = _layer_norm(conv, cg_ref[...], cb_ref[...])
    c = c * _sigmoid(c)
    y = jnp.dot(c.astype(BF16), w2_ref[...], preferred_element_type=F32) + b2_ref[...]
    o_ref[...] = _layer_norm(alpha * x + y, g_ref[...], b_ref[...])


def _conv_block(x, w1, b1, wdw, bdw, cg, cb, w2, b2, g, b, alpha):
    B, S, D = x.shape
    tm = min(ROW_TILE, S)
    assert tm >= HALO_ROWS and HALO_ROWS >= CONV_WIDTH - 1
    row = pl.BlockSpec((None, tm, D), lambda bi, i: (bi, i, 0))
    vec = lambda n: pl.BlockSpec((1, n), lambda bi, i: (0, 0))
    return pl.pallas_call(
        functools.partial(_conv_block_kernel, alpha=alpha),
        grid=(B, S // tm),
        in_specs=[row, _resident(w1.shape), vec(2 * D), _resident(wdw.shape), vec(D), vec(D), vec(D),
                  _resident(w2.shape), vec(D), vec(D), vec(D)],
        out_specs=row,
        out_shape=jax.ShapeDtypeStruct((B, S, D), F32),
        scratch_shapes=[pltpu.VMEM((tm + HALO_ROWS, D), F32)],
        compiler_params=_params(2),
        name="conv_block",
    )(x, w1.astype(BF16), b1.reshape(1, 2 * D), wdw, bdw.reshape(1, D), cg.reshape(1, D),
      cb.reshape(1, D), w2.astype(BF16), b2.reshape(1, D), g.reshape(1, D), b.reshape(1, D))


def kernel(x, attn_w_qkv, attn_w_o, attn_lambda_q1, attn_lambda_k1, attn_lambda_q2, attn_lambda_k2,
           attn_subln_g, conv_w_pw1, conv_b_pw1, conv_w_dw, conv_b_dw, conv_ln_g, conv_ln_b,
           conv_w_pw2, conv_b_pw2, ffn_w_gate, ffn_w_up, ffn_w_down, ln_g, ln_b):
    B, S, D = x.shape
    depth = ln_g.shape[0]
    alpha = (2.0 * depth) ** 0.25
    for i in range(depth):
        j = i // N_MIXERS
        if i % N_MIXERS == 0:
            lambda_init = 0.8 - 0.6 * math.exp(-0.3 * i)
            qt, k, vt = _qkv_rope(x, attn_w_qkv[j])
            lam_vecs = jnp.stack([attn_lambda_q1[j], attn_lambda_k1[j],
                                  attn_lambda_q2[j], attn_lambda_k2[j]]).astype(F32)
            a = _diff_attn(qt, k, vt, lam_vecs, attn_subln_g[j].astype(F32), lambda_init)
            x = _proj_ln(a.reshape(B * S, D), attn_w_o[j], x.reshape(B * S, D),
                         ln_g[i, 0], ln_b[i, 0], alpha).reshape(B, S, D)
        else:
            x = _conv_block(x, conv_w_pw1[j], conv_b_pw1[j], conv_w_dw[j], conv_b_dw[j],
                            conv_ln_g[j], conv_ln_b[j], conv_w_pw2[j], conv_b_pw2[j],
                            ln_g[i, 0], ln_b[i, 0], alpha)
        x = _ffn_ln(x.reshape(B * S, D), ffn_w_gate[i], ffn_w_up[i], ffn_w_down[i],
                    ln_g[i, 1], ln_b[i, 1], alpha).reshape(B, S, D)
    return x
```

```python
import functools
import math

import jax
import jax.numpy as jnp
from jax import lax
from jax.experimental import pallas as pl
from jax.experimental.pallas import tpu as pltpu

N_HEADS = 8
HEAD_DIM = 64
V_DIM = 2 * HEAD_DIM
CHUNK = 64
ROPE_THETA = 10000.0
CONV_WIDTH = 31
LN_EPS = 1e-5
MASK_VALUE = -1e30
N_MIXERS = 2

V7X_VMEM_BYTES = 64 * 1024 * 1024
VMEM_LIMIT_BYTES = V7X_VMEM_BYTES - 8 * 1024 * 1024
SUBLANES = 8
LANES = 128

ROW_TILE = 512
ATTN_TILE = 256
ATTN_HEADS_PER_STEP = 8
HALO_ROWS = 32
LOG2_E = math.log2(math.e)

BF16 = jnp.bfloat16
F32 = jnp.float32


def _params(n_axes):
    return pltpu.CompilerParams(
        dimension_semantics=("arbitrary",) * n_axes,
        vmem_limit_bytes=VMEM_LIMIT_BYTES,
    )


def _resident(shape):
    zeros = (0,) * len(shape)
    return pl.BlockSpec(shape, lambda *_: zeros, pipeline_mode=pl.Buffered(1))


def _layer_norm(z, g, b):
    mu = jnp.mean(z, axis=-1, keepdims=True)
    zc = z - mu
    var = jnp.mean(zc * zc, axis=-1, keepdims=True)
    return zc * lax.rsqrt(var + LN_EPS) * g + b


def _sigmoid(v):
    return 1.0 / (1.0 + jnp.exp(-v))


def _qkv_rope_kernel(x_ref, wqt_ref, wk_ref, wvt_ref, cos_ref, sin_lo_ref, sin_hi_ref,
                     cost_ref, sint_ref, qt_ref, k_ref, vt_ref):
    xb = x_ref[...].astype(BF16)
    nt = (((1,), (1,)), ((), ()))

    k = jnp.dot(xb, wk_ref[...], preferred_element_type=F32)
    cos = cos_ref[...]
    sin_lo = sin_lo_ref[...]
    sin_hi = sin_hi_ref[...]
    half = HEAD_DIM // 2
    for h in range(N_HEADS):
        kh = k[:, h * V_DIM:(h + 1) * V_DIM]
        rot = (kh * cos
               + pltpu.roll(kh, V_DIM - half, 1) * sin_lo
               + pltpu.roll(kh, half, 1) * sin_hi)
        k_ref[:, h * V_DIM:(h + 1) * V_DIM] = rot.astype(BF16)

    qt = lax.dot_general(wqt_ref[...], xb, nt, preferred_element_type=F32)
    cost = cost_ref[...]
    sint = sint_ref[...]
    scale = HEAD_DIM ** -0.5 * LOG2_E
    for g in range(2 * N_HEADS):
        r0 = g * HEAD_DIM
        x1 = qt[r0:r0 + half]
        x2 = qt[r0 + half:r0 + HEAD_DIM]
        qt_ref[r0:r0 + half, :] = ((x1 * cost - x2 * sint) * scale).astype(BF16)
        qt_ref[r0 + half:r0 + HEAD_DIM, :] = ((x2 * cost + x1 * sint) * scale).astype(BF16)

    vt = lax.dot_general(wvt_ref[...], xb, nt, preferred_element_type=F32)
    vt_ref[...] = vt.astype(BF16)


def _qkv_rope(x, w_qkv):
    B, S, D = x.shape
    tm = min(ROW_TILE, S)
    half = HEAD_DIM // 2
    wq, wk, wv = jnp.split(w_qkv, 3, axis=-1)
    wqt = wq.T.astype(BF16)
    wvt = wv.T.astype(BF16)
    wk = wk.astype(BF16)

    pos = jnp.arange(S, dtype=F32)
    inv_freq = ROPE_THETA ** (-jnp.arange(0, HEAD_DIM, 2, dtype=F32) / HEAD_DIM)
    ang = pos[:, None] * inv_freq[None, :]
    cos, sin = jnp.cos(ang), jnp.sin(ang)
    reps = V_DIM // half
    cos_l = jnp.tile(cos, (1, reps))
    sin_l = jnp.tile(sin, (1, reps))
    is_x1 = ((jnp.arange(V_DIM) // half) % 2 == 0)[None, :]
    sin_lo = jnp.where(is_x1, -sin_l, 0.0)
    sin_hi = jnp.where(is_x1, 0.0, sin_l)

    row = lambda b, i: (b, i, 0)
    col = lambda b, i: (b, 0, i)
    tab = pl.BlockSpec((tm, V_DIM), lambda b, i: (i, 0))
    tabt = pl.BlockSpec((half, tm), lambda b, i: (0, i))
    return pl.pallas_call(
        _qkv_rope_kernel,
        grid=(B, S // tm),
        in_specs=[pl.BlockSpec((None, tm, D), row),
                  _resident((D, D)), _resident((D, D)), _resident((D, D)),
                  tab, tab, tab, tabt, tabt],
        out_specs=[pl.BlockSpec((None, D, tm), col),
                   pl.BlockSpec((None, tm, D), row),
                   pl.BlockSpec((None, D, tm), col)],
        out_shape=[jax.ShapeDtypeStruct((B, D, S), BF16),
                   jax.ShapeDtypeStruct((B, S, D), BF16),
                   jax.ShapeDtypeStruct((B, D, S), BF16)],
        compiler_params=_params(2),
        name="qkv_rope",
    )(x, wqt, wk, wvt, cos_l, sin_lo, sin_hi, cos.T, sin.T)


def _diff_attn_kernel(qt_ref, k_ref, vt_ref, lam_ref, g_ref, o_ref,
                      qbd_ref, acc_ref, m_ref, l_ref, *, lambda_init):
    t = ATTN_TILE
    hps = ATTN_HEADS_PER_STEP
    qi = pl.program_id(2)

    zero = jnp.zeros((HEAD_DIM, t), BF16)
    for g in range(hps):
        q = qt_ref[g * V_DIM:(g + 1) * V_DIM, :]
        qbd_ref[g, :HEAD_DIM, :t] = q[:HEAD_DIM]
        qbd_ref[g, :HEAD_DIM, t:] = zero
        qbd_ref[g, HEAD_DIM:, :t] = zero
        qbd_ref[g, HEAD_DIM:, t:] = q[HEAD_DIM:]
    m_ref[...] = jnp.full(m_ref.shape, MASK_VALUE, F32)
    l_ref[...] = jnp.zeros(l_ref.shape, F32)
    acc_ref[...] = jnp.zeros(acc_ref.shape, F32)

    def step(ki, diagonal):
        k0 = pl.multiple_of(ki * t, t)
        lanes = [slice(g * V_DIM, (g + 1) * V_DIM) for g in range(hps)]
        scores = [jnp.dot(k_ref[pl.ds(k0, t), lanes[g]], qbd_ref[g], preferred_element_type=F32)
                  for g in range(hps)]
        vts = [vt_ref[lanes[g], pl.ds(k0, t)] for g in range(hps)]
        stats = [(m_ref[g], l_ref[g], acc_ref[g]) for g in range(hps)]
        results = []
        for g in range(hps):
            s = scores[g]
            m_old, l_old, acc_old = stats[g]
            if diagonal:
                key_chunk = lax.broadcasted_iota(jnp.int32, s.shape, 0) // CHUNK
                q_chunk = (lax.broadcasted_iota(jnp.int32, s.shape, 1) % t) // CHUNK
                s = jnp.where(key_chunk <= q_chunk, s, MASK_VALUE)
            m_new = jnp.maximum(m_old, jnp.max(s, axis=0, keepdims=True))
            alpha = jnp.exp2(m_old - m_new)
            p = jnp.exp2(s - m_new)
            l_new = alpha * l_old + jnp.sum(p, axis=0, keepdims=True)
            pv = jnp.dot(vts[g], p.astype(BF16), preferred_element_type=F32)
            results.append((m_new, l_new, alpha * acc_old + pv))
        for g in range(hps):
            m_ref[g], l_ref[g], acc_ref[g] = results[g]

    def body(ki, carry):
        step(ki, False)
        return carry

    lax.fori_loop(0, qi, body, 0)
    step(qi, True)

    lq = lam_ref[...]
    lam = (jnp.exp(jnp.sum(lq[0:1] * lq[1:2], axis=1, keepdims=True))
           - jnp.exp(jnp.sum(lq[2:3] * lq[3:4], axis=1, keepdims=True))
           + lambda_init)
    for g in range(hps):
        o = acc_ref[g] / l_ref[g]
        d = o[:, :t] - lam * o[:, t:]
        ms = jnp.mean(d * d, axis=0, keepdims=True)
        y = d * lax.rsqrt(ms + LN_EPS) * g_ref[...] * (1.0 - lambda_init)
        o_ref[:, g * V_DIM:(g + 1) * V_DIM] = y.T.astype(BF16)


def _diff_attn(qt, k, vt, lam_vecs, subln_g, lambda_init):
    B, D, S = qt.shape
    t = ATTN_TILE
    hps = ATTN_HEADS_PER_STEP
    w = hps * V_DIM
    assert S % t == 0 and t % CHUNK == 0 and N_HEADS % hps == 0
    return pl.pallas_call(
        functools.partial(_diff_attn_kernel, lambda_init=lambda_init),
        grid=(B, N_HEADS // hps, S // t),
        in_specs=[pl.BlockSpec((None, w, t), lambda b, h, i: (b, h, i)),
                  pl.BlockSpec((None, S, w), lambda b, h, i: (b, 0, h), pipeline_mode=pl.Buffered(1)),
                  pl.BlockSpec((None, w, S), lambda b, h, i: (b, h, 0), pipeline_mode=pl.Buffered(1)),
                  pl.BlockSpec((4, HEAD_DIM), lambda b, h, i: (0, 0)),
                  pl.BlockSpec((V_DIM, 1), lambda b, h, i: (0, 0))],
        out_specs=pl.BlockSpec((None, t, w), lambda b, h, i: (b, i, h)),
        out_shape=jax.ShapeDtypeStruct((B, S, D), BF16),
        scratch_shapes=[pltpu.VMEM((hps, V_DIM, 2 * t), BF16),
                        pltpu.VMEM((hps, V_DIM, 2 * t), F32),
                        pltpu.VMEM((hps, 1, 2 * t), F32),
                        pltpu.VMEM((hps, 1, 2 * t), F32)],
        compiler_params=_params(3),
        name="diff_attn",
    )(qt, k, vt, lam_vecs, subln_g.reshape(V_DIM, 1))


def _proj_ln_kernel(a_ref, w_ref, x_ref, g_ref, b_ref, o_ref, *, alpha):
    y = jnp.dot(a_ref[...], w_ref[...], preferred_element_type=F32)
    o_ref[...] = _layer_norm(alpha * x_ref[...] + y, g_ref[...], b_ref[...])


def _proj_ln(a, w, x, g, b, alpha):
    N, D = x.shape
    tm = min(ROW_TILE, N)
    row = pl.BlockSpec((tm, D), lambda i: (i, 0))
    vec = pl.BlockSpec((1, D), lambda i: (0, 0))
    return pl.pallas_call(
        functools.partial(_proj_ln_kernel, alpha=alpha),
        grid=(N // tm,),
        in_specs=[pl.BlockSpec((tm, a.shape[1]), lambda i: (i, 0)), _resident(w.shape), row, vec, vec],
        out_specs=row,
        out_shape=jax.ShapeDtypeStruct((N, D), F32),
        compiler_params=_params(1),
        name="proj_ln",
    )(a, w.astype(BF16), x, g.reshape(1, D), b.reshape(1, D))


def _ffn_ln_kernel(x_ref, wg_ref, wu_ref, wd_ref, g_ref, b_ref, o_ref, *, alpha):
    x = x_ref[...]
    xb = x.astype(BF16)
    gate = jnp.dot(xb, wg_ref[...], preferred_element_type=F32)
    up = jnp.dot(xb, wu_ref[...], preferred_element_type=F32)
    h = (gate * _sigmoid(gate)) * up
    f = jnp.dot(h.astype(BF16), wd_ref[...], preferred_element_type=F32)
    o_ref[...] = _layer_norm(alpha * x + f, g_ref[...], b_ref[...])


def _ffn_ln(x, wg, wu, wd, g, b, alpha):
    N, D = x.shape
    tm = min(ROW_TILE, N)
    row = pl.BlockSpec((tm, D), lambda i: (i, 0))
    vec = pl.BlockSpec((1, D), lambda i: (0, 0))
    return pl.pallas_call(
        functools.partial(_ffn_ln_kernel, alpha=alpha),
        grid=(N // tm,),
        in_specs=[row, _resident(wg.shape), _resident(wu.shape), _resident(wd.shape), vec, vec],
        out_specs=row,
        out_shape=jax.ShapeDtypeStruct((N, D), F32),
        compiler_params=_params(1),
        name="ffn_ln",
    )(x, wg.astype(BF16), wu.astype(BF16), wd.astype(BF16), g.reshape(1, D), b.reshape(1, D))


def _conv_block_kernel(x_ref, w1_ref, b1_ref, wdw_ref, bdw_ref, cg_ref, cb_ref, w2_ref, b2_ref,
                       g_ref, b_ref, o_ref, hist_ref, *, alpha):
    tm, D = x_ref.shape
    x = x_ref[...]
    h = jnp.dot(x.astype(BF16), w1_ref[...], preferred_element_type=F32) + b1_ref[...]
    glu = h[:, :D] * _sigmoid(h[:, D:])

    @pl.when(pl.program_id(1) == 0)
    def _():
        hist_ref[:HALO_ROWS, :] = jnp.zeros((HALO_ROWS, D), F32)

    hist_ref[HALO_ROWS:, :] = glu

    base = HALO_ROWS - (CONV_WIDTH - 1)
    wdw = wdw_ref[...]
    acc = jnp.zeros((tm, D), F32)
    for r in range(SUBLANES):
        taps = [j for j in range(r, CONV_WIDTH, SUBLANES)]
        span = tm + taps[-1] - r
        shifted = hist_ref[base + r:base + r + span, :]
        for j in taps:
            acc = acc + wdw[j:j + 1, :] * shifted[j - r:j - r + tm, :]
    conv = acc + bdw_ref[...]

    hist_ref[:HALO_ROWS, :] = hist_ref[tm:tm + HALO_ROWS, :]

    c = _layer_norm(conv, cg_ref[...], cb_ref[...])
    c = c * _sigmoid(c)
    y = jnp.dot(c.astype(BF16), w2_ref[...], preferred_element_type=F32) + b2_ref[...]
    o_ref[...] = _layer_norm(alpha * x + y, g_ref[...], b_ref[...])


def _conv_block(x, w1, b1, wdw, bdw, cg, cb, w2, b2, g, b, alpha):
    B, S, D = x.shape
    tm = min(ROW_TILE, S)
    assert tm >= HALO_ROWS and HALO_ROWS >= CONV_WIDTH - 1
    row = pl.BlockSpec((None, tm, D), lambda bi, i: (bi, i, 0))
    vec = lambda n: pl.BlockSpec((1, n), lambda bi, i: (0, 0))
    return pl.pallas_call(
        functools.partial(_conv_block_kernel, alpha=alpha),
        grid=(B, S // tm),
        in_specs=[row, _resident(w1.shape), vec(2 * D), _resident(wdw.shape), vec(D), vec(D), vec(D),
                  _resident(w2.shape), vec(D), vec(D), vec(D)],
        out_specs=row,
        out_shape=jax.ShapeDtypeStruct((B, S, D), F32),
        scratch_shapes=[pltpu.VMEM((tm + HALO_ROWS, D), F32)],
        compiler_params=_params(2),
        name="conv_block",
    )(x, w1.astype(BF16), b1.reshape(1, 2 * D), wdw, bdw.reshape(1, D), cg.reshape(1, D),
      cb.reshape(1, D), w2.astype(BF16), b2.reshape(1, D), g.reshape(1, D), b.reshape(1, D))


def kernel(x, attn_w_qkv, attn_w_o, attn_lambda_q1, attn_lambda_k1, attn_lambda_q2, attn_lambda_k2,
           attn_subln_g, conv_w_pw1, conv_b_pw1, conv_w_dw, conv_b_dw, conv_ln_g, conv_ln_b,
           conv_w_pw2, conv_b_pw2, ffn_w_gate, ffn_w_up, ffn_w_down, ln_g, ln_b):
    B, S, D = x.shape
    depth = ln_g.shape[0]
    alpha = (2.0 * depth) ** 0.25
    for i in range(depth):
        j = i // N_MIXERS
        if i % N_MIXERS == 0:
            lambda_init = 0.8 - 0.6 * math.exp(-0.3 * i)
            qt, k, vt = _qkv_rope(x, attn_w_qkv[j])
            lam_vecs = jnp.stack([attn_lambda_q1[j], attn_lambda_k1[j],
                                  attn_lambda_q2[j], attn_lambda_k2[j]]).astype(F32)
            a = _diff_attn(qt, k, vt, lam_vecs, attn_subln_g[j].astype(F32), lambda_init)
            x = _proj_ln(a.reshape(B * S, D), attn_w_o[j], x.reshape(B * S, D),
                         ln_g[i, 0], ln_b[i, 0], alpha).reshape(B, S, D)
        else:
            x = _conv_block(x, conv_w_pw1[j], conv_b_pw1[j], conv_w_dw[j], conv_b_dw[j],
                            conv_ln_g[j], conv_ln_b[j], conv_w_pw2[j], conv_b_pw2[j],
                            ln_g[i, 0], ln_b[i, 0], alpha)
        x = _ffn_ln(x.reshape(B * S, D), ffn_w_gate[i], ffn_w_up[i], ffn_w_down[i],
                    ln_g[i, 1], ln_b[i, 1], alpha).reshape(B, S, D)
    return x
```

```python
import functools
import math

import jax
import jax.numpy as jnp
from jax import lax
from jax.experimental import pallas as pl
from jax.experimental.pallas import tpu as pltpu

N_HEADS = 8
HEAD_DIM = 64
V_DIM = 2 * HEAD_DIM
V_AUG = V_DIM + 16
CHUNK = 64
ROPE_THETA = 10000.0
CONV_WIDTH = 31
LN_EPS = 1e-5
MASK_VALUE = -1e30
N_MIXERS = 2

V7X_VMEM_BYTES = 64 * 1024 * 1024
VMEM_LIMIT_BYTES = V7X_VMEM_BYTES - 8 * 1024 * 1024
SUBLANES = 8
LANES = 128

ROW_TILE = 512
ATTN_TILE = 256
ATTN_HEADS_PER_STEP = 8
HALO_ROWS = 32
CONV_ROWS = 64
LOG2_E = math.log2(math.e)

BF16 = jnp.bfloat16
F32 = jnp.float32


def _params(n_axes):
    return pltpu.CompilerParams(
        dimension_semantics=("arbitrary",) * n_axes,
        vmem_limit_bytes=VMEM_LIMIT_BYTES,
    )


def _resident(shape):
    zeros = (0,) * len(shape)
    return pl.BlockSpec(shape, lambda *_: zeros, pipeline_mode=pl.Buffered(1))


def _layer_norm(z, g, b):
    mu = jnp.mean(z, axis=-1, keepdims=True)
    zc = z - mu
    var = jnp.mean(zc * zc, axis=-1, keepdims=True)
    return zc * lax.rsqrt(var + LN_EPS) * g + b


def _sigmoid(v):
    return 1.0 / (1.0 + jnp.exp(-v))


def _qkv_rope_kernel(x_ref, wqt_ref, wk_ref, wvt_ref, cos_ref, sin_lo_ref, sin_hi_ref,
                     cost_ref, sint_ref, qt_ref, k_ref, vt_ref):
    xb = x_ref[...].astype(BF16)
    nt = (((1,), (1,)), ((), ()))

    k = jnp.dot(xb, wk_ref[...], preferred_element_type=F32)
    cos = cos_ref[...]
    sin_lo = sin_lo_ref[...]
    sin_hi = sin_hi_ref[...]
    half = HEAD_DIM // 2
    for h in range(N_HEADS):
        kh = k[:, h * V_DIM:(h + 1) * V_DIM]
        rot = (kh * cos
               + pltpu.roll(kh, V_DIM - half, 1) * sin_lo
               + pltpu.roll(kh, half, 1) * sin_hi)
        k_ref[:, h * V_DIM:(h + 1) * V_DIM] = rot.astype(BF16)

    qt = lax.dot_general(wqt_ref[...], xb, nt, preferred_element_type=F32)
    cost = cost_ref[...]
    sint = sint_ref[...]
    scale = HEAD_DIM ** -0.5 * LOG2_E
    for g in range(2 * N_HEADS):
        r0 = g * HEAD_DIM
        x1 = qt[r0:r0 + half]
        x2 = qt[r0 + half:r0 + HEAD_DIM]
        qt_ref[r0:r0 + half, :] = ((x1 * cost - x2 * sint) * scale).astype(BF16)
        qt_ref[r0 + half:r0 + HEAD_DIM, :] = ((x2 * cost + x1 * sint) * scale).astype(BF16)

    vt = lax.dot_general(wvt_ref[...], xb, nt, preferred_element_type=F32)
    tm = vt.shape[1]
    ones_rows = (lax.broadcasted_iota(jnp.int32, (V_AUG - V_DIM, tm), 0) == 0).astype(BF16)
    for h in range(N_HEADS):
        vt_ref[h, :V_DIM, :] = vt[h * V_DIM:(h + 1) * V_DIM].astype(BF16)
        vt_ref[h, V_DIM:, :] = ones_rows


def _qkv_rope(x, w_qkv):
    B, S, D = x.shape
    tm = min(ROW_TILE, S)
    half = HEAD_DIM // 2
    wq, wk, wv = jnp.split(w_qkv, 3, axis=-1)
    wqt = wq.T.astype(BF16)
    wvt = wv.T.astype(BF16)
    wk = wk.astype(BF16)

    pos = jnp.arange(S, dtype=F32)
    inv_freq = ROPE_THETA ** (-jnp.arange(0, HEAD_DIM, 2, dtype=F32) / HEAD_DIM)
    ang = pos[:, None] * inv_freq[None, :]
    cos, sin = jnp.cos(ang), jnp.sin(ang)
    reps = V_DIM // half
    cos_l = jnp.tile(cos, (1, reps))
    sin_l = jnp.tile(sin, (1, reps))
    is_x1 = ((jnp.arange(V_DIM) // half) % 2 == 0)[None, :]
    sin_lo = jnp.where(is_x1, -sin_l, 0.0)
    sin_hi = jnp.where(is_x1, 0.0, sin_l)

    row = lambda b, i: (b, i, 0)
    col = lambda b, i: (b, 0, i)
    tab = pl.BlockSpec((tm, V_DIM), lambda b, i: (i, 0))
    tabt = pl.BlockSpec((half, tm), lambda b, i: (0, i))
    return pl.pallas_call(
        _qkv_rope_kernel,
        grid=(B, S // tm),
        in_specs=[pl.BlockSpec((None, tm, D), row),
                  _resident((D, D)), _resident((D, D)), _resident((D, D)),
                  tab, tab, tab, tabt, tabt],
        out_specs=[pl.BlockSpec((None, D, tm), col),
                   pl.BlockSpec((None, tm, D), row),
                   pl.BlockSpec((None, N_HEADS, V_AUG, tm), lambda b, i: (b, 0, 0, i))],
        out_shape=[jax.ShapeDtypeStruct((B, D, S), BF16),
                   jax.ShapeDtypeStruct((B, S, D), BF16),
                   jax.ShapeDtypeStruct((B, N_HEADS, V_AUG, S), BF16)],
        compiler_params=_params(2),
        name="qkv_rope",
    )(x, wqt, wk, wvt, cos_l, sin_lo, sin_hi, cos.T, sin.T)


def _diff_attn_kernel(qt_ref, k_ref, vt_ref, lam_ref, g_ref, o_ref,
                      qbd_ref, s_ref, acc_ref, m_ref, *, lambda_init):
    t = ATTN_TILE
    hps = ATTN_HEADS_PER_STEP
    qi = pl.program_id(2)

    zero = jnp.zeros((HEAD_DIM, t), BF16)
    for g in range(hps):
        q = qt_ref[g * V_DIM:(g + 1) * V_DIM, :]
        qbd_ref[g, :HEAD_DIM, :t] = q[:HEAD_DIM]
        qbd_ref[g, :HEAD_DIM, t:] = zero
        qbd_ref[g, HEAD_DIM:, :t] = zero
        qbd_ref[g, HEAD_DIM:, t:] = q[HEAD_DIM:]
    m_ref[...] = jnp.full(m_ref.shape, MASK_VALUE, F32)
    acc_ref[...] = jnp.zeros(acc_ref.shape, F32)

    lanes = [slice(g * V_DIM, (g + 1) * V_DIM) for g in range(hps)]

    def scores(ki, g):
        k0 = pl.multiple_of(ki * t, t)
        s_ref[g] = jnp.dot(k_ref[pl.ds(k0, t), lanes[g]], qbd_ref[g],
                           preferred_element_type=F32)

    def softmax_pv(ki, g, diagonal):
        k0 = pl.multiple_of(ki * t, t)
        s = s_ref[g]
        if diagonal:
            key_chunk = lax.broadcasted_iota(jnp.int32, s.shape, 0) // CHUNK
            q_chunk = (lax.broadcasted_iota(jnp.int32, s.shape, 1) % t) // CHUNK
            s = jnp.where(key_chunk <= q_chunk, s, MASK_VALUE)
        s3 = s.reshape(t // SUBLANES, SUBLANES, 2 * t)
        m_old = m_ref[g]
        m_tile = jnp.max(s3, axis=0)
        for shift in (4, 2, 1):
            m_tile = jnp.maximum(m_tile, pltpu.roll(m_tile, shift, 0))
        m_new = jnp.maximum(m_old, m_tile)
        alpha = jnp.exp2(m_old - m_new)
        p = jnp.exp2(s3 - m_new).reshape(t, 2 * t).astype(BF16)
        m_ref[g] = m_new
        pv = jnp.dot(vt_ref[g, :, pl.ds(k0, t)], p, preferred_element_type=F32)
        acc3 = acc_ref[g].reshape(V_AUG // SUBLANES, SUBLANES, 2 * t)
        acc_ref[g] = (alpha * acc3).reshape(V_AUG, 2 * t) + pv

    for g in range(hps):
        scores(0, g)

    def body(ki, carry):
        for g in range(hps):
            softmax_pv(ki, g, False)
            scores(ki + 1, g)
        return carry

    lax.fori_loop(0, qi, body, 0)
    for g in range(hps):
        softmax_pv(qi, g, True)

    lq = lam_ref[...]
    lam = (jnp.exp(jnp.sum(lq[0:1] * lq[1:2], axis=1, keepdims=True))
           - jnp.exp(jnp.sum(lq[2:3] * lq[3:4], axis=1, keepdims=True))
           + lambda_init)
    for g in range(hps):
        o = acc_ref[g, :V_DIM, :] / acc_ref[g, V_DIM:V_DIM + 1, :]
        d = o[:, :t] - lam * o[:, t:]
        ms = jnp.mean(d * d, axis=0, keepdims=True)
        y = d * lax.rsqrt(ms + LN_EPS) * g_ref[...] * (1.0 - lambda_init)
        o_ref[:, g * V_DIM:(g + 1) * V_DIM] = y.T.astype(BF16)


def _diff_attn(qt, k, vt, lam_vecs, subln_g, lambda_init):
    B, D, S = qt.shape
    t = ATTN_TILE
    hps = ATTN_HEADS_PER_STEP
    w = hps * V_DIM
    assert S % t == 0 and t % CHUNK == 0 and N_HEADS % hps == 0
    return pl.pallas_call(
        functools.partial(_diff_attn_kernel, lambda_init=lambda_init),
        grid=(B, N_HEADS // hps, S // t),
        in_specs=[pl.BlockSpec((None, w, t), lambda b, h, i: (b, h, i)),
                  pl.BlockSpec((None, S, w), lambda b, h, i: (b, 0, h), pipeline_mode=pl.Buffered(1)),
                  pl.BlockSpec((None, hps, V_AUG, S), lambda b, h, i: (b, h, 0, 0),
                               pipeline_mode=pl.Buffered(1)),
                  pl.BlockSpec((4, HEAD_DIM), lambda b, h, i: (0, 0)),
                  pl.BlockSpec((V_DIM, 1), lambda b, h, i: (0, 0))],
        out_specs=pl.BlockSpec((None, t, w), lambda b, h, i: (b, i, h)),
        out_shape=jax.ShapeDtypeStruct((B, S, D), BF16),
        scratch_shapes=[pltpu.VMEM((hps, V_DIM, 2 * t), BF16),
                        pltpu.VMEM((hps, t, 2 * t), F32),
                        pltpu.VMEM((hps, V_AUG, 2 * t), F32),
                        pltpu.VMEM((hps, SUBLANES, 2 * t), F32)],
        compiler_params=_params(3),
        name="diff_attn",
    )(qt, k, vt, lam_vecs, subln_g.reshape(V_DIM, 1))


def _proj_ln_kernel(a_ref, w_ref, x_ref, g_ref, b_ref, o_ref, *, alpha):
    y = jnp.dot(a_ref[...], w_ref[...], preferred_element_type=F32)
    o_ref[...] = _layer_norm(alpha * x_ref[...] + y, g_ref[...], b_ref[...])


def _proj_ln(a, w, x, g, b, alpha):
    N, D = x.shape
    tm = min(ROW_TILE, N)
    row = pl.BlockSpec((tm, D), lambda i: (i, 0))
    vec = pl.BlockSpec((1, D), lambda i: (0, 0))
    return pl.pallas_call(
        functools.partial(_proj_ln_kernel, alpha=alpha),
        grid=(N // tm,),
        in_specs=[pl.BlockSpec((tm, a.shape[1]), lambda i: (i, 0)), _resident(w.shape), row, vec, vec],
        out_specs=row,
        out_shape=jax.ShapeDtypeStruct((N, D), F32),
        compiler_params=_params(1),
        name="proj_ln",
    )(a, w.astype(BF16), x, g.reshape(1, D), b.reshape(1, D))


def _ffn_ln_kernel(x_ref, wg_ref, wu_ref, wd_ref, g_ref, b_ref, o_ref, *, alpha):
    x = x_ref[...]
    xb = x.astype(BF16)
    gate = jnp.dot(xb, wg_ref[...], preferred_element_type=F32)
    up = jnp.dot(xb, wu_ref[...], preferred_element_type=F32)
    h = (gate * _sigmoid(gate)) * up
    f = jnp.dot(h.astype(BF16), wd_ref[...], preferred_element_type=F32)
    o_ref[...] = _layer_norm(alpha * x + f, g_ref[...], b_ref[...])


def _ffn_ln(x, wg, wu, wd, g, b, alpha):
    N, D = x.shape
    tm = min(ROW_TILE, N)
    row = pl.BlockSpec((tm, D), lambda i: (i, 0))
    vec = pl.BlockSpec((1, D), lambda i: (0, 0))
    return pl.pallas_call(
        functools.partial(_ffn_ln_kernel, alpha=alpha),
        grid=(N // tm,),
        in_specs=[row, _resident(wg.shape), _resident(wu.shape), _resident(wd.shape), vec, vec],
        out_specs=row,
        out_shape=jax.ShapeDtypeStruct((N, D), F32),
        compiler_params=_params(1),
        name="ffn_ln",
    )(x, wg.astype(BF16), wu.astype(BF16), wd.astype(BF16), g.reshape(1, D), b.reshape(1, D))


def _conv_block_kernel(x_ref, w1_ref, b1_ref, wdw_ref, bdw_ref, cg_ref, cb_ref, w2_ref, b2_ref,
                       g_ref, b_ref, o_ref, hist_ref, conv_ref, *, alpha):
    tm, D = x_ref.shape
    x = x_ref[...]
    h = jnp.dot(x.astype(BF16), w1_ref[...], preferred_element_type=F32) + b1_ref[...]
    glu = h[:, :D] * _sigmoid(h[:, D:])

    @pl.when(pl.program_id(1) == 0)
    def _():
        hist_ref[:HALO_ROWS, :] = jnp.zeros((HALO_ROWS, D), F32)

    hist_ref[HALO_ROWS:, :] = glu

    base = HALO_ROWS - (CONV_WIDTH - 1)
    win = CONV_ROWS + HALO_ROWS

    def conv_chunk(ci, carry):
        row0 = pl.multiple_of(ci * CONV_ROWS, CONV_ROWS)
        for c in range(D // LANES):
            cols = slice(c * LANES, (c + 1) * LANES)
            a = hist_ref[pl.ds(row0, win), cols]
            w = wdw_ref[:, cols]
            acc = jnp.zeros((CONV_ROWS, LANES), F32)
            for r in range(SUBLANES):
                ar = a if r == 0 else pltpu.roll(a, win - r, 0)
                for o in range(r, base + CONV_WIDTH, SUBLANES):
                    j = o - base
                    if j >= 0:
                        acc = acc + w[j:j + 1, :] * ar[o - r:o - r + CONV_ROWS, :]
            conv_ref[pl.ds(row0, CONV_ROWS), cols] = acc + bdw_ref[:, cols]
        return carry

    lax.fori_loop(0, tm // CONV_ROWS, conv_chunk, 0)
    conv = conv_ref[...]

    hist_ref[:HALO_ROWS, :] = hist_ref[tm:tm + HALO_ROWS, :]

    c = _layer_norm(conv, cg_ref[...], cb_ref[...])
    c = c * _sigmoid(c)
    y = jnp.dot(c.astype(BF16), w2_ref[...], preferred_element_type=F32) + b2_ref[...]
    o_ref[...] = _layer_norm(alpha * x + y, g_ref[...], b_ref[...])


def _conv_block(x, w1, b1, wdw, bdw, cg, cb, w2, b2, g, b, alpha):
    B, S, D = x.shape
    tm = min(ROW_TILE, S)
    assert tm >= HALO_ROWS and HALO_ROWS >= CONV_WIDTH - 1
    row = pl.BlockSpec((None, tm, D), lambda bi, i: (bi, i, 0))
    vec = lambda n: pl.BlockSpec((1, n), lambda bi, i: (0, 0))
    return pl.pallas_call(
        functools.partial(_conv_block_kernel, alpha=alpha),
        grid=(B, S // tm),
        in_specs=[row, _resident(w1.shape), vec(2 * D), _resident(wdw.shape), vec(D), vec(D), vec(D),
                  _resident(w2.shape), vec(D), vec(D), vec(D)],
        out_specs=row,
        out_shape=jax.ShapeDtypeStruct((B, S, D), F32),
        scratch_shapes=[pltpu.VMEM((tm + HALO_ROWS, D), F32), pltpu.VMEM((tm, D), F32)],
        compiler_params=_params(2),
        name="conv_block",
    )(x, w1.astype(BF16), b1.reshape(1, 2 * D), wdw, bdw.reshape(1, D), cg.reshape(1, D),
      cb.reshape(1, D), w2.astype(BF16), b2.reshape(1, D), g.reshape(1, D), b.reshape(1, D))


def kernel(x, attn_w_qkv, attn_w_o, attn_lambda_q1, attn_lambda_k1, attn_lambda_q2, attn_lambda_k2,
           attn_subln_g, conv_w_pw1, conv_b_pw1, conv_w_dw, conv_b_dw, conv_ln_g, conv_ln_b,
           conv_w_pw2, conv_b_pw2, ffn_w_gate, ffn_w_up, ffn_w_down, ln_g, ln_b):
    B, S, D = x.shape
    depth = ln_g.shape[0]
    alpha = (2.0 * depth) ** 0.25
    for i in range(depth):
        j = i // N_MIXERS
        if i % N_MIXERS == 0:
            lambda_init = 0.8 - 0.6 * math.exp(-0.3 * i)
            qt, k, vt = _qkv_rope(x, attn_w_qkv[j])
            lam_vecs = jnp.stack([attn_lambda_q1[j], attn_lambda_k1[j],
                                  attn_lambda_q2[j], attn_lambda_k2[j]]).astype(F32)
            a = _diff_attn(qt, k, vt, lam_vecs, attn_subln_g[j].astype(F32), lambda_init)
            x = _proj_ln(a.reshape(B * S, D), attn_w_o[j], x.reshape(B * S, D),
                         ln_g[i, 0], ln_b[i, 0], alpha).reshape(B, S, D)
        else:
            x = _conv_block(x, conv_w_pw1[j], conv_b_pw1[j], conv_w_dw[j], conv_b_dw[j],
                            conv_ln_g[j], conv_ln_b[j], conv_w_pw2[j], conv_b_pw2[j],
                            ln_g[i, 0], ln_b[i, 0], alpha)
        x = _ffn_ln(x.reshape(B * S, D), ffn_w_gate[i], ffn_w_up[i], ffn_w_down[i],
                    ln_g[i, 1], ln_b[i, 1], alpha).reshape(B, S, D)
    return x
```

```python
import functools
import math

import jax
import jax.numpy as jnp
from jax import lax
from jax.experimental import pallas as pl
from jax.experimental.pallas import tpu as pltpu

N_HEADS = 8
HEAD_DIM = 64
V_DIM = 2 * HEAD_DIM
V_AUG = V_DIM + 16
CHUNK = 64
ROPE_THETA = 10000.0
CONV_WIDTH = 31
LN_EPS = 1e-5
MASK_VALUE = -1e30
N_MIXERS = 2

V7X_VMEM_BYTES = 64 * 1024 * 1024
VMEM_LIMIT_BYTES = V7X_VMEM_BYTES - 4 * 1024 * 1024
SUBLANES = 8
LANES = 128

ROW_TILE = 512
ATTN_TILE = 256
ATTN_HEADS_PER_STEP = 8
HALO_ROWS = 32
CONV_ROWS = 64
FFN_SLICE = 256
LOG2_E = math.log2(math.e)

BF16 = jnp.bfloat16
F32 = jnp.float32


def _params(n_axes):
    return pltpu.CompilerParams(
        dimension_semantics=("arbitrary",) * n_axes,
        vmem_limit_bytes=VMEM_LIMIT_BYTES,
    )


def _resident(shape):
    zeros = (0,) * len(shape)
    return pl.BlockSpec(shape, lambda *_: zeros, pipeline_mode=pl.Buffered(1))


def _layer_norm(z, g, b):
    mu = jnp.mean(z, axis=-1, keepdims=True)
    zc = z - mu
    var = jnp.mean(zc * zc, axis=-1, keepdims=True)
    return zc * lax.rsqrt(var + LN_EPS) * g + b


def _sigmoid(v):
    return 1.0 / (1.0 + jnp.exp(-v))


def _qkv_rope_kernel(x_ref, wqt_ref, wk_ref, wvt_ref, cos_ref, sin_lo_ref, sin_hi_ref,
                     cost_ref, sint_ref, qt_ref, k_ref, vt_ref):
    xb = x_ref[...].astype(BF16)
    nt = (((1,), (1,)), ((), ()))

    k = jnp.dot(xb, wk_ref[...], preferred_element_type=F32)
    cos = cos_ref[...]
    sin_lo = sin_lo_ref[...]
    sin_hi = sin_hi_ref[...]
    half = HEAD_DIM // 2
    for h in range(N_HEADS):
        kh = k[:, h * V_DIM:(h + 1) * V_DIM]
        rot = (kh * cos
               + pltpu.roll(kh, V_DIM - half, 1) * sin_lo
               + pltpu.roll(kh, half, 1) * sin_hi)
        k_ref[:, h * V_DIM:(h + 1) * V_DIM] = rot.astype(BF16)

    qt = lax.dot_general(wqt_ref[...], xb, nt, preferred_element_type=F32)
    cost = cost_ref[...]
    sint = sint_ref[...]
    scale = HEAD_DIM ** -0.5 * LOG2_E
    for g in range(2 * N_HEADS):
        r0 = g * HEAD_DIM
        x1 = qt[r0:r0 + half]
        x2 = qt[r0 + half:r0 + HEAD_DIM]
        qt_ref[r0:r0 + half, :] = ((x1 * cost - x2 * sint) * scale).astype(BF16)
        qt_ref[r0 + half:r0 + HEAD_DIM, :] = ((x2 * cost + x1 * sint) * scale).astype(BF16)

    vt = lax.dot_general(wvt_ref[...], xb, nt, preferred_element_type=F32)
    tm = vt.shape[1]
    ones_rows = (lax.broadcasted_iota(jnp.int32, (V_AUG - V_DIM, tm), 0) == 0).astype(BF16)
    for h in range(N_HEADS):
        vt_ref[h, :V_DIM, :] = vt[h * V_DIM:(h + 1) * V_DIM].astype(BF16)
        vt_ref[h, V_DIM:, :] = ones_rows


def _qkv_rope(x, w_qkv):
    B, S, D = x.shape
    tm = min(ROW_TILE, S)
    half = HEAD_DIM // 2
    wq, wk, wv = jnp.split(w_qkv, 3, axis=-1)
    wqt = wq.T.astype(BF16)
    wvt = wv.T.astype(BF16)
    wk = wk.astype(BF16)

    pos = jnp.arange(S, dtype=F32)
    inv_freq = ROPE_THETA ** (-jnp.arange(0, HEAD_DIM, 2, dtype=F32) / HEAD_DIM)
    ang = pos[:, None] * inv_freq[None, :]
    cos, sin = jnp.cos(ang), jnp.sin(ang)
    reps = V_DIM // half
    cos_l = jnp.tile(cos, (1, reps))
    sin_l = jnp.tile(sin, (1, reps))
    is_x1 = ((jnp.arange(V_DIM) // half) % 2 == 0)[None, :]
    sin_lo = jnp.where(is_x1, -sin_l, 0.0)
    sin_hi = jnp.where(is_x1, 0.0, sin_l)

    row = lambda b, i: (b, i, 0)
    col = lambda b, i: (b, 0, i)
    tab = pl.BlockSpec((tm, V_DIM), lambda b, i: (i, 0))
    tabt = pl.BlockSpec((half, tm), lambda b, i: (0, i))
    return pl.pallas_call(
        _qkv_rope_kernel,
        grid=(B, S // tm),
        in_specs=[pl.BlockSpec((None, tm, D), row),
                  _resident((D, D)), _resident((D, D)), _resident((D, D)),
                  tab, tab, tab, tabt, tabt],
        out_specs=[pl.BlockSpec((None, D, tm), col),
                   pl.BlockSpec((None, tm, D), row),
                   pl.BlockSpec((None, N_HEADS, V_AUG, tm), lambda b, i: (b, 0, 0, i))],
        out_shape=[jax.ShapeDtypeStruct((B, D, S), BF16),
                   jax.ShapeDtypeStruct((B, S, D), BF16),
                   jax.ShapeDtypeStruct((B, N_HEADS, V_AUG, S), BF16)],
        compiler_params=_params(2),
        name="qkv_rope",
    )(x, wqt, wk, wvt, cos_l, sin_lo, sin_hi, cos.T, sin.T)


def _diff_attn_kernel(qt_ref, k_ref, vt_ref, lam_ref, g_ref, o_ref,
                      qbd_ref, s_ref, acc_ref, m_ref, *, lambda_init):
    t = ATTN_TILE
    hps = ATTN_HEADS_PER_STEP
    qi = pl.program_id(2)

    zero = jnp.zeros((HEAD_DIM, t), BF16)
    for g in range(hps):
        q = qt_ref[g * V_DIM:(g + 1) * V_DIM, :]
        qbd_ref[g, :HEAD_DIM, :t] = q[:HEAD_DIM]
        qbd_ref[g, :HEAD_DIM, t:] = zero
        qbd_ref[g, HEAD_DIM:, :t] = zero
        qbd_ref[g, HEAD_DIM:, t:] = q[HEAD_DIM:]
    m_ref[...] = jnp.full(m_ref.shape, MASK_VALUE, F32)
    acc_ref[...] = jnp.zeros(acc_ref.shape, F32)

    lanes = [slice(g * V_DIM, (g + 1) * V_DIM) for g in range(hps)]

    def scores(ki, g):
        k0 = pl.multiple_of(ki * t, t)
        s_ref[g] = jnp.dot(k_ref[pl.ds(k0, t), lanes[g]], qbd_ref[g],
                           preferred_element_type=F32)

    def softmax(g, diagonal):
        s = s_ref[g]
        if diagonal:
            key_chunk = lax.broadcasted_iota(jnp.int32, s.shape, 0) // CHUNK
            q_chunk = (lax.broadcasted_iota(jnp.int32, s.shape, 1) % t) // CHUNK
            s = jnp.where(key_chunk <= q_chunk, s, MASK_VALUE)
        s3 = s.reshape(t // SUBLANES, SUBLANES, 2 * t)
        m_old = m_ref[g]
        m_tile = jnp.max(s3, axis=0)
        for shift in (4, 2, 1):
            m_tile = jnp.maximum(m_tile, pltpu.roll(m_tile, shift, 0))
        m_new = jnp.maximum(m_old, m_tile)
        alpha = jnp.exp2(m_old - m_new)
        p = jnp.exp2(s3 - m_new).reshape(t, 2 * t).astype(BF16)
        m_ref[g] = m_new
        return p, alpha

    def pv_update(ki, g, p, alpha):
        k0 = pl.multiple_of(ki * t, t)
        pv = jnp.dot(vt_ref[g, :, pl.ds(k0, t)], p, preferred_element_type=F32)
        acc3 = acc_ref[g].reshape(V_AUG // SUBLANES, SUBLANES, 2 * t)
        acc_ref[g] = (alpha * acc3).reshape(V_AUG, 2 * t) + pv

    for g in range(hps):
        scores(0, g)

    def body(ki, carry):
        for g in range(hps):
            p, alpha = softmax(g, False)
            scores(ki + 1, g)
            pv_update(ki, g, p, alpha)
        return carry

    lax.fori_loop(0, qi, body, 0)
    for g in range(hps):
        p, alpha = softmax(g, True)
        pv_update(qi, g, p, alpha)

    lq = lam_ref[...]
    lam = (jnp.exp(jnp.sum(lq[0:1] * lq[1:2], axis=1, keepdims=True))
           - jnp.exp(jnp.sum(lq[2:3] * lq[3:4], axis=1, keepdims=True))
           + lambda_init)
    for g in range(hps):
        o = acc_ref[g, :V_DIM, :] / acc_ref[g, V_DIM:V_DIM + 1, :]
        d = o[:, :t] - lam * o[:, t:]
        ms = jnp.mean(d * d, axis=0, keepdims=True)
        y = d * lax.rsqrt(ms + LN_EPS) * g_ref[...] * (1.0 - lambda_init)
        o_ref[:, g * V_DIM:(g + 1) * V_DIM] = y.T.astype(BF16)


def _diff_attn(qt, k, vt, lam_vecs, subln_g, lambda_init):
    B, D, S = qt.shape
    t = ATTN_TILE
    hps = ATTN_HEADS_PER_STEP
    w = hps * V_DIM
    assert S % t == 0 and t % CHUNK == 0 and N_HEADS % hps == 0
    return pl.pallas_call(
        functools.partial(_diff_attn_kernel, lambda_init=lambda_init),
        grid=(B, N_HEADS // hps, S // t),
        in_specs=[pl.BlockSpec((None, w, t), lambda b, h, i: (b, h, i)),
                  pl.BlockSpec((None, S, w), lambda b, h, i: (b, 0, h), pipeline_mode=pl.Buffered(1)),
                  pl.BlockSpec((None, hps, V_AUG, S), lambda b, h, i: (b, h, 0, 0),
                               pipeline_mode=pl.Buffered(1)),
                  pl.BlockSpec((4, HEAD_DIM), lambda b, h, i: (0, 0)),
                  pl.BlockSpec((V_DIM, 1), lambda b, h, i: (0, 0))],
        out_specs=pl.BlockSpec((None, t, w), lambda b, h, i: (b, i, h)),
        out_shape=jax.ShapeDtypeStruct((B, S, D), BF16),
        scratch_shapes=[pltpu.VMEM((hps, V_DIM, 2 * t), BF16),
                        pltpu.VMEM((hps, t, 2 * t), F32),
                        pltpu.VMEM((hps, V_AUG, 2 * t), F32),
                        pltpu.VMEM((hps, SUBLANES, 2 * t), F32)],
        compiler_params=_params(3),
        name="diff_attn",
    )(qt, k, vt, lam_vecs, subln_g.reshape(V_DIM, 1))


def _proj_ffn_kernel(a_ref, wo_ref, x_ref, g1_ref, b1_ref, wg_ref, wu_ref, wd_ref, g2_ref, b2_ref,
                     o_ref, *, alpha):
    y = jnp.dot(a_ref[...], wo_ref[...], preferred_element_type=F32)
    x1 = _layer_norm(alpha * x_ref[...] + y, g1_ref[...], b1_ref[...])
    xb = x1.astype(BF16)
    gate = jnp.dot(xb, wg_ref[...], preferred_element_type=F32)
    up = jnp.dot(xb, wu_ref[...], preferred_element_type=F32)
    h = (gate * _sigmoid(gate)) * up
    f = jnp.dot(h.astype(BF16), wd_ref[...], preferred_element_type=F32)
    o_ref[...] = _layer_norm(alpha * x1 + f, g2_ref[...], b2_ref[...])


def _proj_ffn(a, wo, x, g1, b1, wg, wu, wd, g2, b2, alpha):
    N, D = x.shape
    tm = min(ROW_TILE, N)
    row = pl.BlockSpec((tm, D), lambda i: (i, 0))
    vec = pl.BlockSpec((1, D), lambda i: (0, 0))
    return pl.pallas_call(
        functools.partial(_proj_ffn_kernel, alpha=alpha),
        grid=(N // tm,),
        in_specs=[pl.BlockSpec((tm, a.shape[1]), lambda i: (i, 0)), _resident(wo.shape), row, vec, vec,
                  _resident(wg.shape), _resident(wu.shape), _resident(wd.shape), vec, vec],
        out_specs=row,
        out_shape=jax.ShapeDtypeStruct((N, D), F32),
        compiler_params=_params(1),
        name="proj_ffn",
    )(a, wo.astype(BF16), x, g1.reshape(1, D), b1.reshape(1, D),
      wg.astype(BF16), wu.astype(BF16), wd.astype(BF16), g2.reshape(1, D), b2.reshape(1, D))


def _conv_block_kernel(x_ref, w1_ref, b1_ref, wdw_ref, bdw_ref, cg_ref, cb_ref, w2_ref, b2_ref,
                       g_ref, b_ref, o_ref, hist_ref, conv_ref, *, alpha):
    tm, D = x_ref.shape
    x = x_ref[...]
    h = jnp.dot(x.astype(BF16), w1_ref[...], preferred_element_type=F32) + b1_ref[...]
    glu = h[:, :D] * _sigmoid(h[:, D:])

    @pl.when(pl.program_id(1) == 0)
    def _():
        hist_ref[:HALO_ROWS, :] = jnp.zeros((HALO_ROWS, D), F32)

    hist_ref[HALO_ROWS:, :] = glu

    base = HALO_ROWS - (CONV_WIDTH - 1)
    win = CONV_ROWS + HALO_ROWS

    def conv_chunk(ci, carry):
        row0 = pl.multiple_of(ci * CONV_ROWS, CONV_ROWS)
        for c in range(D // LANES):
            cols = slice(c * LANES, (c + 1) * LANES)
            a = hist_ref[pl.ds(row0, win), cols]
            w = wdw_ref[:, cols]
            acc = jnp.zeros((CONV_ROWS, LANES), F32)
            for r in range(SUBLANES):
                ar = a if r == 0 else pltpu.roll(a, win - r, 0)
                for o in range(r, base + CONV_WIDTH, SUBLANES):
                    j = o - base
                    if j >= 0:
                        acc = acc + w[j:j + 1, :] * ar[o - r:o - r + CONV_ROWS, :]
            conv_ref[pl.ds(row0, CONV_ROWS), cols] = acc + bdw_ref[:, cols]
        return carry

    lax.fori_loop(0, tm // CONV_ROWS, conv_chunk, 0)
    conv = conv_ref[...]

    hist_ref[:HALO_ROWS, :] = hist_ref[tm:tm + HALO_ROWS, :]

    c = _layer_norm(conv, cg_ref[...], cb_ref[...])
    c = c * _sigmoid(c)
    y = jnp.dot(c.astype(BF16), w2_ref[...], preferred_element_type=F32) + b2_ref[...]
    o_ref[...] = _layer_norm(alpha * x + y, g_ref[...], b_ref[...])


def _conv_block(x, w1, b1, wdw, bdw, cg, cb, w2, b2, g, b, alpha):
    B, S, D = x.shape
    tm = min(ROW_TILE, S)
    assert tm >= HALO_ROWS and HALO_ROWS >= CONV_WIDTH - 1
    row = pl.BlockSpec((None, tm, D), lambda bi, i: (bi, i, 0))
    vec = lambda n: pl.BlockSpec((1, n), lambda bi, i: (0, 0))
    return pl.pallas_call(
        functools.partial(_conv_block_kernel, alpha=alpha),
        grid=(B, S // tm),
        in_specs=[row, _resident(w1.shape), vec(2 * D), _resident(wdw.shape), vec(D), vec(D), vec(D),
                  _resident(w2.shape), vec(D), vec(D), vec(D)],
        out_specs=row,
        out_shape=jax.ShapeDtypeStruct((B, S, D), F32),
        scratch_shapes=[pltpu.VMEM((tm + HALO_ROWS, D), F32), pltpu.VMEM((tm, D), F32)],
        compiler_params=_params(2),
        name="conv_block",
    )(x, w1.astype(BF16), b1.reshape(1, 2 * D), wdw, bdw.reshape(1, D), cg.reshape(1, D),
      cb.reshape(1, D), w2.astype(BF16), b2.reshape(1, D), g.reshape(1, D), b.reshape(1, D))


def _exact_zero_from(v):
    bits = pltpu.bitcast(v, jnp.uint32)
    bits = lax.shift_right_logical(lax.shift_right_logical(bits, jnp.uint32(16)), jnp.uint32(16))
    return pltpu.bitcast(bits, F32)


def _ffn_slice(xb, wg_ref, wu_ref, wd_ref, s):
    cols = slice(s * FFN_SLICE, (s + 1) * FFN_SLICE)
    gate = jnp.dot(xb, wg_ref[:, cols], preferred_element_type=F32)
    up = jnp.dot(xb, wu_ref[:, cols], preferred_element_type=F32)
    h = (gate * _sigmoid(gate)) * up
    return jnp.dot(h.astype(BF16), wd_ref[cols, :], preferred_element_type=F32)


def _conv_ffn_kernel(x_ref, w1_ref, b1_ref, wdw_ref, bdw_ref, cg_ref, cb_ref, w2_ref, b2_ref,
                     g1_ref, bb1_ref, wg_ref, wu_ref, wd_ref, g2_ref, bb2_ref, o_ref,
                     hist_ref, conv_ref, mid_ref, *, alpha, tiles_per_seq):
    i = pl.program_id(0)
    tm, D = x_ref.shape
    n_chunks = tm // CONV_ROWS
    n_slices = wg_ref.shape[1] // FFN_SLICE

    @pl.when(i % tiles_per_seq == 0)
    def _():
        hist_ref[:HALO_ROWS, :] = jnp.zeros((HALO_ROWS, D), F32)

    @pl.when(i == 0)
    def _():
        mid_ref[...] = jnp.zeros(mid_ref.shape, F32)

    xm = mid_ref[...]
    xmb = xm.astype(BF16)

    x = x_ref[...]
    h = jnp.dot(x.astype(BF16), w1_ref[...], preferred_element_type=F32) + b1_ref[...]
    hist_ref[HALO_ROWS:, :] = h[:, :D] * _sigmoid(h[:, D:])

    base = HALO_ROWS - (CONV_WIDTH - 1)
    win = CONV_ROWS + HALO_ROWS
    f = jnp.zeros((tm, D), F32)
    acc0 = jnp.zeros((CONV_ROWS, LANES), F32)
    for ci in range(n_chunks):
        row0 = ci * CONV_ROWS
        for c in range(D // LANES):
            cols = slice(c * LANES, (c + 1) * LANES)
            a = hist_ref[row0:row0 + win, cols]
            w = wdw_ref[:, cols]
            acc = acc0
            for r in range(SUBLANES):
                ar = a if r == 0 else pltpu.roll(a, win - r, 0)
                for o in range(r, base + CONV_WIDTH, SUBLANES):
                    j = o - base
                    if j >= 0:
                        acc = acc + w[j:j + 1, :] * ar[o - r:o - r + CONV_ROWS, :]
            conv_ref[row0:row0 + CONV_ROWS, cols] = acc + bdw_ref[:, cols]
        if ci < n_slices:
            fs = _ffn_slice(xmb, wg_ref, wu_ref, wd_ref, ci)
            f = f + fs
            acc0 = jnp.concatenate([_exact_zero_from(fs[:SUBLANES, :LANES])] * (CONV_ROWS // SUBLANES),
                                   axis=0)

    hist_ref[:HALO_ROWS, :] = hist_ref[tm:tm + HALO_ROWS, :]
    c = _layer_norm(conv_ref[...], cg_ref[...], cb_ref[...])
    c = c * _sigmoid(c)
    y = jnp.dot(c.astype(BF16), w2_ref[...], preferred_element_type=F32) + b2_ref[...]
    for s in range(min(n_chunks, n_slices), n_slices):
        f = f + _ffn_slice(xmb, wg_ref, wu_ref, wd_ref, s)
    mid_ref[...] = _layer_norm(alpha * x + y, g1_ref[...], bb1_ref[...])
    o_ref[...] = _layer_norm(alpha * xm + f, g2_ref[...], bb2_ref[...])


def _conv_ffn(x, w1, b1, wdw, bdw, cg, cb, w2, b2, g1, bb1, wg, wu, wd, g2, bb2, alpha):
    B, S, D = x.shape
    tm = min(ROW_TILE, S)
    assert S % tm == 0 and tm % CONV_ROWS == 0 and HALO_ROWS >= CONV_WIDTH - 1
    assert wg.shape[1] % FFN_SLICE == 0
    n_tiles = B * S // tm
    vec = lambda n: pl.BlockSpec((1, n), lambda i: (0, 0))
    out = pl.pallas_call(
        functools.partial(_conv_ffn_kernel, alpha=alpha, tiles_per_seq=S // tm),
        grid=(n_tiles + 1,),
        in_specs=[pl.BlockSpec((tm, D), lambda i: (jnp.minimum(i, n_tiles - 1), 0)),
                  _resident(w1.shape), vec(2 * D), _resident(wdw.shape), vec(D), vec(D), vec(D),
                  _resident(w2.shape), vec(D), vec(D), vec(D),
                  _resident(wg.shape), _resident(wu.shape), _resident(wd.shape), vec(D), vec(D)],
        out_specs=pl.BlockSpec((tm, D), lambda i: (jnp.maximum(i - 1, 0), 0)),
        out_shape=jax.ShapeDtypeStruct((B * S, D), F32),
        scratch_shapes=[pltpu.VMEM((tm + HALO_ROWS, D), F32), pltpu.VMEM((tm, D), F32),
                        pltpu.VMEM((tm, D), F32)],
        compiler_params=_params(1),
        name="conv_ffn",
    )(x.reshape(B * S, D), w1.astype(BF16), b1.reshape(1, 2 * D), wdw, bdw.reshape(1, D),
      cg.reshape(1, D), cb.reshape(1, D), w2.astype(BF16), b2.reshape(1, D),
      g1.reshape(1, D), bb1.reshape(1, D), wg.astype(BF16), wu.astype(BF16), wd.astype(BF16),
      g2.reshape(1, D), bb2.reshape(1, D))
    return out.reshape(B, S, D)


def kernel(x, attn_w_qkv, attn_w_o, attn_lambda_q1, attn_lambda_k1, attn_lambda_q2, attn_lambda_k2,
           attn_subln_g, conv_w_pw1, conv_b_pw1, conv_w_dw, conv_b_dw, conv_ln_g, conv_ln_b,
           conv_w_pw2, conv_b_pw2, ffn_w_gate, ffn_w_up, ffn_w_down, ln_g, ln_b):
    B, S, D = x.shape
    depth = ln_g.shape[0]
    alpha = (2.0 * depth) ** 0.25
    for i in range(depth):
        j = i // N_MIXERS
        if i % N_MIXERS == 0:
            lambda_init = 0.8 - 0.6 * math.exp(-0.3 * i)
            qt, k, vt = _qkv_rope(x, attn_w_qkv[j])
            lam_vecs = jnp.stack([attn_lambda_q1[j], attn_lambda_k1[j],
                                  attn_lambda_q2[j], attn_lambda_k2[j]]).astype(F32)
            a = _diff_attn(qt, k, vt, lam_vecs, attn_subln_g[j].astype(F32), lambda_init)
            x = _proj_ffn(a.reshape(B * S, D), attn_w_o[j], x.reshape(B * S, D),
                          ln_g[i, 0], ln_b[i, 0], ffn_w_gate[i], ffn_w_up[i], ffn_w_down[i],
                          ln_g[i, 1], ln_b[i, 1], alpha).reshape(B, S, D)
        else:
            x = _conv_ffn(x, conv_w_pw1[j], conv_b_pw1[j], conv_w_dw[j], conv_b_dw[j],
                          conv_ln_g[j], conv_ln_b[j], conv_w_pw2[j], conv_b_pw2[j],
                          ln_g[i, 0], ln_b[i, 0], ffn_w_gate[i], ffn_w_up[i], ffn_w_down[i],
                          ln_g[i, 1], ln_b[i, 1], alpha)
    return x
```

```python
import functools
import math

import jax
import jax.numpy as jnp
from jax import lax
from jax.experimental import pallas as pl
from jax.experimental.pallas import tpu as pltpu

N_HEADS = 8
HEAD_DIM = 64
V_DIM = 2 * HEAD_DIM
V_AUG = V_DIM + 16
CHUNK = 64
ROPE_THETA = 10000.0
CONV_WIDTH = 31
LN_EPS = 1e-5
MASK_VALUE = -1e30
N_MIXERS = 2

V7X_VMEM_BYTES = 64 * 1024 * 1024
VMEM_LIMIT_BYTES = V7X_VMEM_BYTES - 4 * 1024 * 1024
SUBLANES = 8
LANES = 128

ROW_TILE = 512
ATTN_Q_TILE = 256
ATTN_K_TILE = 256
ATTN_HEADS_PER_STEP = 8
HALO_ROWS = 32
CONV_ROWS = 64
FFN_SLICE = 256
LOG2_E = math.log2(math.e)

BF16 = jnp.bfloat16
F32 = jnp.float32


def _params(n_axes):
    return pltpu.CompilerParams(
        dimension_semantics=("arbitrary",) * n_axes,
        vmem_limit_bytes=VMEM_LIMIT_BYTES,
    )


def _resident(shape):
    zeros = (0,) * len(shape)
    return pl.BlockSpec(shape, lambda *_: zeros, pipeline_mode=pl.Buffered(1))


def _layer_norm(z, g, b):
    mu = jnp.mean(z, axis=-1, keepdims=True)
    zc = z - mu
    var = jnp.mean(zc * zc, axis=-1, keepdims=True)
    return zc * lax.rsqrt(var + LN_EPS) * g + b


def _sigmoid(v):
    return 1.0 / (1.0 + jnp.exp(-v))


def _qkv_rope_kernel(x_ref, wqt_ref, wk_ref, wvt_ref, cos_ref, sin_lo_ref, sin_hi_ref,
                     cost_ref, sint_ref, qt_ref, k_ref, vt_ref):
    xb = x_ref[...].astype(BF16)
    nt = (((1,), (1,)), ((), ()))

    k = jnp.dot(xb, wk_ref[...], preferred_element_type=F32)
    cos = cos_ref[...]
    sin_lo = sin_lo_ref[...]
    sin_hi = sin_hi_ref[...]
    half = HEAD_DIM // 2
    for h in range(N_HEADS):
        kh = k[:, h * V_DIM:(h + 1) * V_DIM]
        rot = (kh * cos
               + pltpu.roll(kh, V_DIM - half, 1) * sin_lo
               + pltpu.roll(kh, half, 1) * sin_hi)
        k_ref[:, h * V_DIM:(h + 1) * V_DIM] = rot.astype(BF16)

    qt = lax.dot_general(wqt_ref[...], xb, nt, preferred_element_type=F32)
    cost = cost_ref[...]
    sint = sint_ref[...]
    scale = HEAD_DIM ** -0.5 * LOG2_E
    for g in range(2 * N_HEADS):
        r0 = g * HEAD_DIM
        x1 = qt[r0:r0 + half]
        x2 = qt[r0 + half:r0 + HEAD_DIM]
        qt_ref[r0:r0 + half, :] = ((x1 * cost - x2 * sint) * scale).astype(BF16)
        qt_ref[r0 + half:r0 + HEAD_DIM, :] = ((x2 * cost + x1 * sint) * scale).astype(BF16)

    vt = lax.dot_general(wvt_ref[...], xb, nt, preferred_element_type=F32)
    tm = vt.shape[1]
    ones_rows = (lax.broadcasted_iota(jnp.int32, (V_AUG - V_DIM, tm), 0) == 0).astype(BF16)
    for h in range(N_HEADS):
        vt_ref[h, :V_DIM, :] = vt[h * V_DIM:(h + 1) * V_DIM].astype(BF16)
        vt_ref[h, V_DIM:, :] = ones_rows


def _qkv_rope(x, w_qkv):
    B, S, D = x.shape
    tm = min(ROW_TILE, S)
    half = HEAD_DIM // 2
    wq, wk, wv = jnp.split(w_qkv, 3, axis=-1)
    wqt = wq.T.astype(BF16)
    wvt = wv.T.astype(BF16)
    wk = wk.astype(BF16)

    pos = jnp.arange(S, dtype=F32)
    inv_freq = ROPE_THETA ** (-jnp.arange(0, HEAD_DIM, 2, dtype=F32) / HEAD_DIM)
    ang = pos[:, None] * inv_freq[None, :]
    cos, sin = jnp.cos(ang), jnp.sin(ang)
    reps = V_DIM // half
    cos_l = jnp.tile(cos, (1, reps))
    sin_l = jnp.tile(sin, (1, reps))
    is_x1 = ((jnp.arange(V_DIM) // half) % 2 == 0)[None, :]
    sin_lo = jnp.where(is_x1, -sin_l, 0.0)
    sin_hi = jnp.where(is_x1, 0.0, sin_l)

    row = lambda b, i: (b, i, 0)
    col = lambda b, i: (b, 0, i)
    tab = pl.BlockSpec((tm, V_DIM), lambda b, i: (i, 0))
    tabt = pl.BlockSpec((half, tm), lambda b, i: (0, i))
    return pl.pallas_call(
        _qkv_rope_kernel,
        grid=(B, S // tm),
        in_specs=[pl.BlockSpec((None, tm, D), row),
                  _resident((D, D)), _resident((D, D)), _resident((D, D)),
                  tab, tab, tab, tabt, tabt],
        out_specs=[pl.BlockSpec((None, D, tm), col),
                   pl.BlockSpec((None, tm, D), row),
                   pl.BlockSpec((None, N_HEADS, V_AUG, tm), lambda b, i: (b, 0, 0, i))],
        out_shape=[jax.ShapeDtypeStruct((B, D, S), BF16),
                   jax.ShapeDtypeStruct((B, S, D), BF16),
                   jax.ShapeDtypeStruct((B, N_HEADS, V_AUG, S), BF16)],
        compiler_params=_params(2),
        name="qkv_rope",
    )(x, wqt, wk, wvt, cos_l, sin_lo, sin_hi, cos.T, sin.T)


def _diff_attn_kernel(qt_ref, k_ref, vt_ref, lam_ref, g_ref, o_ref,
                      qbd_ref, s_ref, acc_ref, m_ref, *, lambda_init):
    tq = ATTN_Q_TILE
    tk = ATTN_K_TILE
    n_diag = tq // tk
    hps = ATTN_HEADS_PER_STEP
    qi = pl.program_id(2)

    zero = jnp.zeros((HEAD_DIM, tq), BF16)
    for g in range(hps):
        q = qt_ref[g * V_DIM:(g + 1) * V_DIM, :]
        qbd_ref[g, :HEAD_DIM, :tq] = q[:HEAD_DIM]
        qbd_ref[g, :HEAD_DIM, tq:] = zero
        qbd_ref[g, HEAD_DIM:, :tq] = zero
        qbd_ref[g, HEAD_DIM:, tq:] = q[HEAD_DIM:]
    m_ref[...] = jnp.full(m_ref.shape, MASK_VALUE, F32)
    acc_ref[...] = jnp.zeros(acc_ref.shape, F32)

    lanes = [slice(g * V_DIM, (g + 1) * V_DIM) for g in range(hps)]

    def scores(ki, g):
        k0 = pl.multiple_of(ki * tk, tk)
        s_ref[g] = jnp.dot(k_ref[pl.ds(k0, tk), lanes[g]], qbd_ref[g],
                           preferred_element_type=F32)

    def softmax(g, key_offset):
        s = s_ref[g]
        if key_offset is not None:
            key_chunk = (lax.broadcasted_iota(jnp.int32, s.shape, 0) + key_offset) // CHUNK
            q_chunk = (lax.broadcasted_iota(jnp.int32, s.shape, 1) % tq) // CHUNK
            s = jnp.where(key_chunk <= q_chunk, s, MASK_VALUE)
        s3 = s.reshape(tk // SUBLANES, SUBLANES, 2 * tq)
        m_old = m_ref[g]
        m_tile = jnp.max(s3, axis=0)
        for shift in (4, 2, 1):
            m_tile = jnp.maximum(m_tile, pltpu.roll(m_tile, shift, 0))
        m_new = jnp.maximum(m_old, m_tile)
        alpha = jnp.exp2(m_old - m_new)
        p = jnp.exp2(s3 - m_new).reshape(tk, 2 * tq).astype(BF16)
        m_ref[g] = m_new
        return p, alpha

    def pv_update(ki, g, p, alpha):
        k0 = pl.multiple_of(ki * tk, tk)
        pv = jnp.dot(vt_ref[g, :, pl.ds(k0, tk)], p, preferred_element_type=F32)
        acc3 = acc_ref[g].reshape(V_AUG // SUBLANES, SUBLANES, 2 * tq)
        acc_ref[g] = (alpha * acc3).reshape(V_AUG, 2 * tq) + pv

    def key_tile(ki, key_offset, more):
        for g in range(hps):
            p, alpha = softmax(g, key_offset)
            if more:
                scores(ki + 1, g)
            pv_update(ki, g, p, alpha)

    for g in range(hps):
        scores(0, g)

    def body(ki, carry):
        key_tile(ki, None, True)
        return carry

    n_full = qi * n_diag
    lax.fori_loop(0, n_full, body, 0)
    for j in range(n_diag):
        key_tile(n_full + j, j * tk, j + 1 < n_diag)

    lq = lam_ref[...]
    lam = (jnp.exp(jnp.sum(lq[0:1] * lq[1:2], axis=1, keepdims=True))
           - jnp.exp(jnp.sum(lq[2:3] * lq[3:4], axis=1, keepdims=True))
           + lambda_init)
    for g in range(hps):
        o = acc_ref[g, :V_DIM, :] / acc_ref[g, V_DIM:V_DIM + 1, :]
        d = o[:, :tq] - lam * o[:, tq:]
        ms = jnp.mean(d * d, axis=0, keepdims=True)
        y = d * lax.rsqrt(ms + LN_EPS) * g_ref[...] * (1.0 - lambda_init)
        o_ref[:, g * V_DIM:(g + 1) * V_DIM] = y.T.astype(BF16)


def _diff_attn(qt, k, vt, lam_vecs, subln_g, lambda_init):
    B, D, S = qt.shape
    tq = min(ATTN_Q_TILE, S)
    tk = ATTN_K_TILE
    hps = ATTN_HEADS_PER_STEP
    w = hps * V_DIM
    assert tq == ATTN_Q_TILE and S % tq == 0 and tq % tk == 0 and tk % CHUNK == 0
    assert N_HEADS % hps == 0
    return pl.pallas_call(
        functools.partial(_diff_attn_kernel, lambda_init=lambda_init),
        grid=(B, N_HEADS // hps, S // tq),
        in_specs=[pl.BlockSpec((None, w, tq), lambda b, h, i: (b, h, i)),
                  pl.BlockSpec((None, S, w), lambda b, h, i: (b, 0, h), pipeline_mode=pl.Buffered(1)),
                  pl.BlockSpec((None, hps, V_AUG, S), lambda b, h, i: (b, h, 0, 0),
                               pipeline_mode=pl.Buffered(1)),
                  pl.BlockSpec((4, HEAD_DIM), lambda b, h, i: (0, 0)),
                  pl.BlockSpec((V_DIM, 1), lambda b, h, i: (0, 0))],
        out_specs=pl.BlockSpec((None, tq, w), lambda b, h, i: (b, i, h)),
        out_shape=jax.ShapeDtypeStruct((B, S, D), BF16),
        scratch_shapes=[pltpu.VMEM((hps, V_DIM, 2 * tq), BF16),
                        pltpu.VMEM((hps, tk, 2 * tq), F32),
                        pltpu.VMEM((hps, V_AUG, 2 * tq), F32),
                        pltpu.VMEM((hps, SUBLANES, 2 * tq), F32)],
        compiler_params=_params(3),
        name="diff_attn",
    )(qt, k, vt, lam_vecs, subln_g.reshape(V_DIM, 1))


def _proj_ffn_kernel(a_ref, wo_ref, x_ref, g1_ref, b1_ref, wg_ref, wu_ref, wd_ref, g2_ref, b2_ref,
                     o_ref, *, alpha):
    y = jnp.dot(a_ref[...], wo_ref[...], preferred_element_type=F32)
    x1 = _layer_norm(alpha * x_ref[...] + y, g1_ref[...], b1_ref[...])
    xb = x1.astype(BF16)
    gate = jnp.dot(xb, wg_ref[...], preferred_element_type=F32)
    up = jnp.dot(xb, wu_ref[...], preferred_element_type=F32)
    h = (gate * _sigmoid(gate)) * up
    f = jnp.dot(h.astype(BF16), wd_ref[...], preferred_element_type=F32)
    o_ref[...] = _layer_norm(alpha * x1 + f, g2_ref[...], b2_ref[...])


def _proj_ffn(a, wo, x, g1, b1, wg, wu, wd, g2, b2, alpha):
    N, D = x.shape
    tm = min(ROW_TILE, N)
    row = pl.BlockSpec((tm, D), lambda i: (i, 0))
    vec = pl.BlockSpec((1, D), lambda i: (0, 0))
    return pl.pallas_call(
        functools.partial(_proj_ffn_kernel, alpha=alpha),
        grid=(N // tm,),
        in_specs=[pl.BlockSpec((tm, a.shape[1]), lambda i: (i, 0)), _resident(wo.shape), row, vec, vec,
                  _resident(wg.shape), _resident(wu.shape), _resident(wd.shape), vec, vec],
        out_specs=row,
        out_shape=jax.ShapeDtypeStruct((N, D), F32),
        compiler_params=_params(1),
        name="proj_ffn",
    )(a, wo.astype(BF16), x, g1.reshape(1, D), b1.reshape(1, D),
      wg.astype(BF16), wu.astype(BF16), wd.astype(BF16), g2.reshape(1, D), b2.reshape(1, D))


def _exact_zero_from(v):
    bits = pltpu.bitcast(v, jnp.uint32)
    bits = lax.shift_right_logical(lax.shift_right_logical(bits, jnp.uint32(16)), jnp.uint32(16))
    return pltpu.bitcast(bits, F32)


def _ffn_slice(xb, wg_ref, wu_ref, wd_ref, s):
    cols = slice(s * FFN_SLICE, (s + 1) * FFN_SLICE)
    gate = jnp.dot(xb, wg_ref[:, cols], preferred_element_type=F32)
    up = jnp.dot(xb, wu_ref[:, cols], preferred_element_type=F32)
    h = (gate * _sigmoid(gate)) * up
    return jnp.dot(h.astype(BF16), wd_ref[cols, :], preferred_element_type=F32)


def _conv_ffn_kernel(x_ref, w1_ref, b1_ref, wdw_ref, bdw_ref, cg_ref, cb_ref, w2_ref, b2_ref,
                     g1_ref, bb1_ref, wg_ref, wu_ref, wd_ref, g2_ref, bb2_ref, o_ref,
                     hist_ref, conv_ref, mid_ref, *, alpha, tiles_per_seq):
    i = pl.program_id(0)
    tm, D = x_ref.shape
    n_chunks = tm // CONV_ROWS
    n_slices = wg_ref.shape[1] // FFN_SLICE

    @pl.when(i % tiles_per_seq == 0)
    def _():
        hist_ref[:HALO_ROWS, :] = jnp.zeros((HALO_ROWS, D), F32)

    @pl.when(i == 0)
    def _():
        mid_ref[...] = jnp.zeros(mid_ref.shape, F32)

    xm = mid_ref[...]
    xmb = xm.astype(BF16)

    x = x_ref[...]
    h = jnp.dot(x.astype(BF16), w1_ref[...], preferred_element_type=F32) + b1_ref[...]
    hist_ref[HALO_ROWS:, :] = h[:, :D] * _sigmoid(h[:, D:])

    base = HALO_ROWS - (CONV_WIDTH - 1)
    win = CONV_ROWS + HALO_ROWS
    f = jnp.zeros((tm, D), F32)
    acc0 = jnp.zeros((CONV_ROWS, LANES), F32)
    for ci in range(n_chunks):
        row0 = ci * CONV_ROWS
        for c in range(D // LANES):
            cols = slice(c * LANES, (c + 1) * LANES)
            a = hist_ref[row0:row0 + win, cols]
            acc = acc0
            for r in range(SUBLANES):
                rows = CONV_ROWS if r == 0 else CONV_ROWS + SUBLANES
                part = None
                for o in range(r, base + CONV_WIDTH, SUBLANES):
                    j = o - base
                    if j >= 0:
                        a3 = a[o - r:o - r + rows, :].reshape(rows // SUBLANES, SUBLANES, LANES)
                        term = (a3 * wdw_ref[j, :, cols]).reshape(rows, LANES)
                        part = term if part is None else part + term
                if r > 0:
                    part = pltpu.roll(part, rows - r, 0)[:CONV_ROWS, :]
                acc = acc + part
            conv_ref[row0:row0 + CONV_ROWS, cols] = acc + bdw_ref[:, cols]
        if ci < n_slices:
            fs = _ffn_slice(xmb, wg_ref, wu_ref, wd_ref, ci)
            f = f + fs
            acc0 = jnp.concatenate([_exact_zero_from(fs[:SUBLANES, :LANES])] * (CONV_ROWS // SUBLANES),
                                   axis=0)

    hist_ref[:HALO_ROWS, :] = hist_ref[tm:tm + HALO_ROWS, :]
    c = _layer_norm(conv_ref[...], cg_ref[...], cb_ref[...])
    c = c * _sigmoid(c)
    y = jnp.dot(c.astype(BF16), w2_ref[...], preferred_element_type=F32) + b2_ref[...]
    for s in range(min(n_chunks, n_slices), n_slices):
        f = f + _ffn_slice(xmb, wg_ref, wu_ref, wd_ref, s)
    mid_ref[...] = _layer_norm(alpha * x + y, g1_ref[...], bb1_ref[...])
    o_ref[...] = _layer_norm(alpha * xm + f, g2_ref[...], bb2_ref[...])


def _conv_ffn(x, w1, b1, wdw, bdw, cg, cb, w2, b2, g1, bb1, wg, wu, wd, g2, bb2, alpha):
    B, S, D = x.shape
    tm = min(ROW_TILE, S)
    assert S % tm == 0 and tm % CONV_ROWS == 0 and HALO_ROWS >= CONV_WIDTH - 1
    assert wg.shape[1] % FFN_SLICE == 0
    n_tiles = B * S // tm
    wdw_rep = jnp.broadcast_to(wdw[:, None, :], (CONV_WIDTH, SUBLANES, D))
    vec = lambda n: pl.BlockSpec((1, n), lambda i: (0, 0))
    out = pl.pallas_call(
        functools.partial(_conv_ffn_kernel, alpha=alpha, tiles_per_seq=S // tm),
        grid=(n_tiles + 1,),
        in_specs=[pl.BlockSpec((tm, D), lambda i: (jnp.minimum(i, n_tiles - 1), 0)),
                  _resident(w1.shape), vec(2 * D), _resident(wdw_rep.shape), vec(D), vec(D), vec(D),
                  _resident(w2.shape), vec(D), vec(D), vec(D),
                  _resident(wg.shape), _resident(wu.shape), _resident(wd.shape), vec(D), vec(D)],
        out_specs=pl.BlockSpec((tm, D), lambda i: (jnp.maximum(i - 1, 0), 0)),
        out_shape=jax.ShapeDtypeStruct((B * S, D), F32),
        scratch_shapes=[pltpu.VMEM((tm + HALO_ROWS, D), F32), pltpu.VMEM((tm, D), F32),
                        pltpu.VMEM((tm, D), F32)],
        compiler_params=_params(1),
        name="conv_ffn",
    )(x.reshape(B * S, D), w1.astype(BF16), b1.reshape(1, 2 * D), wdw_rep, bdw.reshape(1, D),
      cg.reshape(1, D), cb.reshape(1, D), w2.astype(BF16), b2.reshape(1, D),
      g1.reshape(1, D), bb1.reshape(1, D), wg.astype(BF16), wu.astype(BF16), wd.astype(BF16),
      g2.reshape(1, D), bb2.reshape(1, D))
    return out.reshape(B, S, D)


def kernel(x, attn_w_qkv, attn_w_o, attn_lambda_q1, attn_lambda_k1, attn_lambda_q2, attn_lambda_k2,
           attn_subln_g, conv_w_pw1, conv_b_pw1, conv_w_dw, conv_b_dw, conv_ln_g, conv_ln_b,
           conv_w_pw2, conv_b_pw2, ffn_w_gate, ffn_w_up, ffn_w_down, ln_g, ln_b):
    B, S, D = x.shape
    depth = ln_g.shape[0]
    alpha = (2.0 * depth) ** 0.25
    for i in range(depth):
        j = i // N_MIXERS
        if i % N_MIXERS == 0:
            lambda_init = 0.8 - 0.6 * math.exp(-0.3 * i)
            qt, k, vt = _qkv_rope(x, attn_w_qkv[j])
            lam_vecs = jnp.stack([attn_lambda_q1[j], attn_lambda_k1[j],
                                  attn_lambda_q2[j], attn_lambda_k2[j]]).astype(F32)
            a = _diff_attn(qt, k, vt, lam_vecs, attn_subln_g[j].astype(F32), lambda_init)
            x = _proj_ffn(a.reshape(B * S, D), attn_w_o[j], x.reshape(B * S, D),
                          ln_g[i, 0], ln_b[i, 0], ffn_w_gate[i], ffn_w_up[i], ffn_w_down[i],
                          ln_g[i, 1], ln_b[i, 1], alpha).reshape(B, S, D)
        else:
            x = _conv_ffn(x, conv_w_pw1[j], conv_b_pw1[j], conv_w_dw[j], conv_b_dw[j],
                          conv_ln_g[j], conv_ln_b[j], conv_w_pw2[j], conv_b_pw2[j],
                          ln_g[i, 0], ln_b[i, 0], ffn_w_gate[i], ffn_w_up[i], ffn_w_down[i],
                          ln_g[i, 1], ln_b[i, 1], alpha)
    return x
```

```python
import functools
import math

import jax
import jax.numpy as jnp
from jax import lax
from jax.experimental import pallas as pl
from jax.experimental.pallas import tpu as pltpu

N_HEADS = 8
HEAD_DIM = 64
V_DIM = 2 * HEAD_DIM
V_AUG = V_DIM + 16
CHUNK = 64
ROPE_THETA = 10000.0
CONV_WIDTH = 31
LN_EPS = 1e-5
MASK_VALUE = -1e30
N_MIXERS = 2

V7X_VMEM_BYTES = 64 * 1024 * 1024
VMEM_LIMIT_BYTES = V7X_VMEM_BYTES - 4 * 1024 * 1024
SUBLANES = 8
LANES = 128

ROW_TILE = 512
ATTN_Q_TILE = 256
ATTN_K_TILE = 256
ATTN_HEADS_PER_STEP = 8
HALO_ROWS = 32
CONV_ROWS = 64
GLU_COLS = 1024
FFN_SLICE = 256
LOG2_E = math.log2(math.e)

BF16 = jnp.bfloat16
F32 = jnp.float32


def _params(n_axes):
    return pltpu.CompilerParams(
        dimension_semantics=("arbitrary",) * n_axes,
        vmem_limit_bytes=VMEM_LIMIT_BYTES,
    )


def _resident(shape):
    zeros = (0,) * len(shape)
    return pl.BlockSpec(shape, lambda *_: zeros, pipeline_mode=pl.Buffered(1))


def _layer_norm(z, g, b):
    mu = jnp.mean(z, axis=-1, keepdims=True)
    zc = z - mu
    var = jnp.mean(zc * zc, axis=-1, keepdims=True)
    return zc * lax.rsqrt(var + LN_EPS) * g + b


def _sigmoid(v):
    return 1.0 / (1.0 + jnp.exp(-v))


def _qkv_rope_kernel(x_ref, wqt_ref, wk_ref, wvt_ref, cos_ref, sin_lo_ref, sin_hi_ref,
                     cost_ref, sint_ref, qt_ref, k_ref, vt_ref):
    xb = x_ref[...].astype(BF16)
    nt = (((1,), (1,)), ((), ()))

    k = jnp.dot(xb, wk_ref[...], preferred_element_type=F32)
    cos = cos_ref[...]
    sin_lo = sin_lo_ref[...]
    sin_hi = sin_hi_ref[...]
    half = HEAD_DIM // 2
    for h in range(N_HEADS):
        kh = k[:, h * V_DIM:(h + 1) * V_DIM]
        rot = (kh * cos
               + pltpu.roll(kh, V_DIM - half, 1) * sin_lo
               + pltpu.roll(kh, half, 1) * sin_hi)
        k_ref[:, h * V_DIM:(h + 1) * V_DIM] = rot.astype(BF16)

    qt = lax.dot_general(wqt_ref[...], xb, nt, preferred_element_type=F32)
    cost = cost_ref[...]
    sint = sint_ref[...]
    scale = HEAD_DIM ** -0.5 * LOG2_E
    for g in range(2 * N_HEADS):
        r0 = g * HEAD_DIM
        x1 = qt[r0:r0 + half]
        x2 = qt[r0 + half:r0 + HEAD_DIM]
        qt_ref[r0:r0 + half, :] = ((x1 * cost - x2 * sint) * scale).astype(BF16)
        qt_ref[r0 + half:r0 + HEAD_DIM, :] = ((x2 * cost + x1 * sint) * scale).astype(BF16)

    vt = lax.dot_general(wvt_ref[...], xb, nt, preferred_element_type=F32)
    tm = vt.shape[1]
    ones_rows = (lax.broadcasted_iota(jnp.int32, (V_AUG - V_DIM, tm), 0) == 0).astype(BF16)
    for h in range(N_HEADS):
        vt_ref[h, :V_DIM, :] = vt[h * V_DIM:(h + 1) * V_DIM].astype(BF16)
        vt_ref[h, V_DIM:, :] = ones_rows


def _qkv_rope(x, w_qkv):
    B, S, D = x.shape
    tm = min(ROW_TILE, S)
    half = HEAD_DIM // 2
    wq, wk, wv = jnp.split(w_qkv, 3, axis=-1)
    wqt = wq.T.astype(BF16)
    wvt = wv.T.astype(BF16)
    wk = wk.astype(BF16)

    pos = jnp.arange(S, dtype=F32)
    inv_freq = ROPE_THETA ** (-jnp.arange(0, HEAD_DIM, 2, dtype=F32) / HEAD_DIM)
    ang = pos[:, None] * inv_freq[None, :]
    cos, sin = jnp.cos(ang), jnp.sin(ang)
    reps = V_DIM // half
    cos_l = jnp.tile(cos, (1, reps))
    sin_l = jnp.tile(sin, (1, reps))
    is_x1 = ((jnp.arange(V_DIM) // half) % 2 == 0)[None, :]
    sin_lo = jnp.where(is_x1, -sin_l, 0.0)
    sin_hi = jnp.where(is_x1, 0.0, sin_l)

    row = lambda b, i: (b, i, 0)
    col = lambda b, i: (b, 0, i)
    tab = pl.BlockSpec((tm, V_DIM), lambda b, i: (i, 0))
    tabt = pl.BlockSpec((half, tm), lambda b, i: (0, i))
    return pl.pallas_call(
        _qkv_rope_kernel,
        grid=(B, S // tm),
        in_specs=[pl.BlockSpec((None, tm, D), row),
                  _resident((D, D)), _resident((D, D)), _resident((D, D)),
                  tab, tab, tab, tabt, tabt],
        out_specs=[pl.BlockSpec((None, D, tm), col),
                   pl.BlockSpec((None, tm, D), row),
                   pl.BlockSpec((None, N_HEADS, V_AUG, tm), lambda b, i: (b, 0, 0, i))],
        out_shape=[jax.ShapeDtypeStruct((B, D, S), BF16),
                   jax.ShapeDtypeStruct((B, S, D), BF16),
                   jax.ShapeDtypeStruct((B, N_HEADS, V_AUG, S), BF16)],
        compiler_params=_params(2),
        name="qkv_rope",
    )(x, wqt, wk, wvt, cos_l, sin_lo, sin_hi, cos.T, sin.T)


def _diff_attn_kernel(qt_ref, qt_next_ref, k_ref, vt_ref, lam_ref, g_ref, o_ref,
                      qbd_ref, s_ref, acc_ref, m_ref, *, lambda_init):
    tq = ATTN_Q_TILE
    tk = ATTN_K_TILE
    n_diag = tq // tk
    hps = ATTN_HEADS_PER_STEP
    qi = pl.program_id(2)
    lanes = [slice(g * V_DIM, (g + 1) * V_DIM) for g in range(hps)]

    def load_q(src_ref, g):
        q = src_ref[lanes[g], :]
        qbd_ref[g, :HEAD_DIM, :tq] = q[:HEAD_DIM]
        qbd_ref[g, HEAD_DIM:, tq:] = q[HEAD_DIM:]

    def scores(ki, g):
        k0 = pl.multiple_of(ki * tk, tk)
        s_ref[g] = jnp.dot(k_ref[pl.ds(k0, tk), lanes[g]], qbd_ref[g],
                           preferred_element_type=F32)

    def softmax(g, key_offset):
        s = s_ref[g]
        if key_offset is not None:
            key_chunk = (lax.broadcasted_iota(jnp.int32, s.shape, 0) + key_offset) // CHUNK
            q_chunk = (lax.broadcasted_iota(jnp.int32, s.shape, 1) % tq) // CHUNK
            s = jnp.where(key_chunk <= q_chunk, s, MASK_VALUE)
        s3 = s.reshape(tk // SUBLANES, SUBLANES, 2 * tq)
        m_old = m_ref[g]
        m_tile = jnp.max(s3, axis=0)
        for shift in (4, 2, 1):
            m_tile = jnp.maximum(m_tile, pltpu.roll(m_tile, shift, 0))
        m_new = jnp.maximum(m_old, m_tile)
        alpha = jnp.exp2(m_old - m_new)
        p = jnp.exp2(s3 - m_new).reshape(tk, 2 * tq).astype(BF16)
        m_ref[g] = m_new
        return p, alpha

    def pv_update(ki, g, p, alpha):
        k0 = pl.multiple_of(ki * tk, tk)
        pv = jnp.dot(vt_ref[g, :, pl.ds(k0, tk)], p, preferred_element_type=F32)
        acc3 = acc_ref[g].reshape(V_AUG // SUBLANES, SUBLANES, 2 * tq)
        acc_ref[g] = (alpha * acc3).reshape(V_AUG, 2 * tq) + pv

    def key_tile(ki, key_offset, last):
        for g in range(hps):
            p, alpha = softmax(g, key_offset)
            if last:
                load_q(qt_next_ref, g)
                scores(0, g)
            else:
                scores(ki + 1, g)
            pv_update(ki, g, p, alpha)

    @pl.when(qi == 0)
    def _():
        zero = jnp.zeros((HEAD_DIM, tq), BF16)
        for g in range(hps):
            qbd_ref[g, :HEAD_DIM, tq:] = zero
            qbd_ref[g, HEAD_DIM:, :tq] = zero
            load_q(qt_ref, g)
            scores(0, g)

    m_ref[...] = jnp.full(m_ref.shape, MASK_VALUE, F32)
    acc_ref[...] = jnp.zeros(acc_ref.shape, F32)

    def body(ki, carry):
        key_tile(ki, None, False)
        return carry

    n_full = qi * n_diag
    lax.fori_loop(0, n_full, body, 0)
    for j in range(n_diag):
        key_tile(n_full + j, j * tk, j + 1 == n_diag)

    lq = lam_ref[...]
    lam = (jnp.exp(jnp.sum(lq[0:1] * lq[1:2], axis=1, keepdims=True))
           - jnp.exp(jnp.sum(lq[2:3] * lq[3:4], axis=1, keepdims=True))
           + lambda_init)
    for g in range(hps):
        o = acc_ref[g, :V_DIM, :] / acc_ref[g, V_DIM:V_DIM + 1, :]
        d = o[:, :tq] - lam * o[:, tq:]
        ms = jnp.mean(d * d, axis=0, keepdims=True)
        y = d * lax.rsqrt(ms + LN_EPS) * g_ref[...] * (1.0 - lambda_init)
        o_ref[:, g * V_DIM:(g + 1) * V_DIM] = y.T.astype(BF16)


def _diff_attn(qt, k, vt, lam_vecs, subln_g, lambda_init):
    B, D, S = qt.shape
    tq = min(ATTN_Q_TILE, S)
    tk = ATTN_K_TILE
    hps = ATTN_HEADS_PER_STEP
    w = hps * V_DIM
    assert tq == ATTN_Q_TILE and S % tq == 0 and tq % tk == 0 and tk % CHUNK == 0
    assert N_HEADS % hps == 0
    last_q = S // tq - 1
    return pl.pallas_call(
        functools.partial(_diff_attn_kernel, lambda_init=lambda_init),
        grid=(B, N_HEADS // hps, S // tq),
        in_specs=[pl.BlockSpec((None, w, tq), lambda b, h, i: (b, h, i)),
                  pl.BlockSpec((None, w, tq), lambda b, h, i: (b, h, jnp.minimum(i + 1, last_q))),
                  pl.BlockSpec((None, S, w), lambda b, h, i: (b, 0, h), pipeline_mode=pl.Buffered(1)),
                  pl.BlockSpec((None, hps, V_AUG, S), lambda b, h, i: (b, h, 0, 0),
                               pipeline_mode=pl.Buffered(1)),
                  pl.BlockSpec((4, HEAD_DIM), lambda b, h, i: (0, 0)),
                  pl.BlockSpec((V_DIM, 1), lambda b, h, i: (0, 0))],
        out_specs=pl.BlockSpec((None, tq, w), lambda b, h, i: (b, i, h)),
        out_shape=jax.ShapeDtypeStruct((B, S, D), BF16),
        scratch_shapes=[pltpu.VMEM((hps, V_DIM, 2 * tq), BF16),
                        pltpu.VMEM((hps, tk, 2 * tq), F32),
                        pltpu.VMEM((hps, V_AUG, 2 * tq), F32),
                        pltpu.VMEM((hps, SUBLANES, 2 * tq), F32)],
        compiler_params=_params(3),
        name="diff_attn",
    )(qt, qt, k, vt, lam_vecs, subln_g.reshape(V_DIM, 1))


def _proj_ffn_kernel(a_ref, wo_ref, x_ref, g1_ref, b1_ref, wg_ref, wu_ref, wd_ref, g2_ref, b2_ref,
                     o_ref, *, alpha):
    y = jnp.dot(a_ref[...], wo_ref[...], preferred_element_type=F32)
    x1 = _layer_norm(alpha * x_ref[...] + y, g1_ref[...], b1_ref[...])
    xb = x1.astype(BF16)
    gate = jnp.dot(xb, wg_ref[...], preferred_element_type=F32)
    up = jnp.dot(xb, wu_ref[...], preferred_element_type=F32)
    h = (gate * _sigmoid(gate)) * up
    f = jnp.dot(h.astype(BF16), wd_ref[...], preferred_element_type=F32)
    o_ref[...] = _layer_norm(alpha * x1 + f, g2_ref[...], b2_ref[...])


def _proj_ffn(a, wo, x, g1, b1, wg, wu, wd, g2, b2, alpha):
    N, D = x.shape
    tm = min(ROW_TILE, N)
    row = pl.BlockSpec((tm, D), lambda i: (i, 0))
    vec = pl.BlockSpec((1, D), lambda i: (0, 0))
    return pl.pallas_call(
        functools.partial(_proj_ffn_kernel, alpha=alpha),
        grid=(N // tm,),
        in_specs=[pl.BlockSpec((tm, a.shape[1]), lambda i: (i, 0)), _resident(wo.shape), row, vec, vec,
                  _resident(wg.shape), _resident(wu.shape), _resident(wd.shape), vec, vec],
        out_specs=row,
        out_shape=jax.ShapeDtypeStruct((N, D), F32),
        compiler_params=_params(1),
        name="proj_ffn",
    )(a, wo.astype(BF16), x, g1.reshape(1, D), b1.reshape(1, D),
      wg.astype(BF16), wu.astype(BF16), wd.astype(BF16), g2.reshape(1, D), b2.reshape(1, D))


def _exact_zero_from(v):
    bits = pltpu.bitcast(v, jnp.uint32)
    bits = lax.shift_right_logical(lax.shift_right_logical(bits, jnp.uint32(16)), jnp.uint32(16))
    return pltpu.bitcast(bits, F32)


def _ffn_slice(xb, wg_ref, wu_ref, wd_ref, s):
    cols = slice(s * FFN_SLICE, (s + 1) * FFN_SLICE)
    gate = jnp.dot(xb, wg_ref[:, cols], preferred_element_type=F32)
    up = jnp.dot(xb, wu_ref[:, cols], preferred_element_type=F32)
    h = (gate * _sigmoid(gate)) * up
    return jnp.dot(h.astype(BF16), wd_ref[cols, :], preferred_element_type=F32)


def _conv_ffn_kernel(x_ref, w1_ref, b1_ref, wdw_ref, bdw_ref, cg_ref, cb_ref, w2_ref, b2_ref,
                     g1_ref, bb1_ref, wg_ref, wu_ref, wd_ref, g2_ref, bb2_ref, o_ref,
                     hist_ref, conv_ref, mid_ref, *, alpha, tiles_per_seq):
    i = pl.program_id(0)
    tm, D = x_ref.shape
    n_chunks = tm // CONV_ROWS
    n_slices = wg_ref.shape[1] // FFN_SLICE

    @pl.when(i % tiles_per_seq == 0)
    def _():
        hist_ref[:HALO_ROWS, :] = jnp.zeros((HALO_ROWS, D), F32)

    @pl.when(i == 0)
    def _():
        mid_ref[...] = jnp.zeros(mid_ref.shape, F32)

    xm = mid_ref[...]
    xmb = xm.astype(BF16)

    x = x_ref[...]
    xb = x.astype(BF16)

    def glu_block(cb):
        ca = slice(cb * GLU_COLS, (cb + 1) * GLU_COLS)
        cg = slice(D + cb * GLU_COLS, D + (cb + 1) * GLU_COLS)
        ha = jnp.dot(xb, w1_ref[:, ca], preferred_element_type=F32) + b1_ref[:, ca]
        hg = jnp.dot(xb, w1_ref[:, cg], preferred_element_type=F32) + b1_ref[:, cg]
        hist_ref[HALO_ROWS:, ca] = ha * _sigmoid(hg)

    base = HALO_ROWS - (CONV_WIDTH - 1)
    win = CONV_ROWS + HALO_ROWS
    f = jnp.zeros((tm, D), F32)
    acc0 = jnp.zeros((CONV_ROWS, LANES), F32)
    n_cb = D // GLU_COLS
    n_units = n_cb * n_chunks
    for u in range(n_units):
        cb, ci = divmod(u, n_chunks)
        if ci == 0:
            glu_block(cb)
        row0 = ci * CONV_ROWS
        for c in range(cb * GLU_COLS // LANES, (cb + 1) * GLU_COLS // LANES):
            cols = slice(c * LANES, (c + 1) * LANES)
            a = hist_ref[row0:row0 + win, cols]
            acc = acc0
            for r in range(SUBLANES):
                rows = CONV_ROWS if r == 0 else CONV_ROWS + SUBLANES
                part = None
                for o in range(r, base + CONV_WIDTH, SUBLANES):
                    j = o - base
                    if j >= 0:
                        a3 = a[o - r:o - r + rows, :].reshape(rows // SUBLANES, SUBLANES, LANES)
                        term = (a3 * wdw_ref[j, :, cols]).reshape(rows, LANES)
                        part = term if part is None else part + term
                if r > 0:
                    part = pltpu.roll(part, rows - r, 0)[:CONV_ROWS, :]
                acc = acc + part
            conv_ref[row0:row0 + CONV_ROWS, cols] = acc + bdw_ref[:, cols]
        for s in range(u * n_slices // n_units, (u + 1) * n_slices // n_units):
            fs = _ffn_slice(xmb, wg_ref, wu_ref, wd_ref, s)
            f = f + fs
            acc0 = jnp.concatenate([_exact_zero_from(fs[:SUBLANES, :LANES])] * (CONV_ROWS // SUBLANES),
                                   axis=0)

    hist_ref[:HALO_ROWS, :] = hist_ref[tm:tm + HALO_ROWS, :]
    c = _layer_norm(conv_ref[...], cg_ref[...], cb_ref[...])
    c = c * _sigmoid(c)
    y = jnp.dot(c.astype(BF16), w2_ref[...], preferred_element_type=F32) + b2_ref[...]
    mid_ref[...] = _layer_norm(alpha * x + y, g1_ref[...], bb1_ref[...])
    o_ref[...] = _layer_norm(alpha * xm + f, g2_ref[...], bb2_ref[...])


def _conv_ffn(x, w1, b1, wdw, bdw, cg, cb, w2, b2, g1, bb1, wg, wu, wd, g2, bb2, alpha):
    B, S, D = x.shape
    tm = min(ROW_TILE, S)
    assert S % tm == 0 and tm % CONV_ROWS == 0 and HALO_ROWS >= CONV_WIDTH - 1
    assert wg.shape[1] % FFN_SLICE == 0 and D % GLU_COLS == 0
    n_tiles = B * S // tm
    wdw_rep = jnp.broadcast_to(wdw[:, None, :], (CONV_WIDTH, SUBLANES, D))
    vec = lambda n: pl.BlockSpec((1, n), lambda i: (0, 0))
    out = pl.pallas_call(
        functools.partial(_conv_ffn_kernel, alpha=alpha, tiles_per_seq=S // tm),
        grid=(n_tiles + 1,),
        in_specs=[pl.BlockSpec((tm, D), lambda i: (jnp.minimum(i, n_tiles - 1), 0)),
                  _resident(w1.shape), vec(2 * D), _resident(wdw_rep.shape), vec(D), vec(D), vec(D),
                  _resident(w2.shape), vec(D), vec(D), vec(D),
                  _resident(wg.shape), _resident(wu.shape), _resident(wd.shape), vec(D), vec(D)],
        out_specs=pl.BlockSpec((tm, D), lambda i: (jnp.maximum(i - 1, 0), 0)),
        out_shape=jax.ShapeDtypeStruct((B * S, D), F32),
        scratch_shapes=[pltpu.VMEM((tm + HALO_ROWS, D), F32), pltpu.VMEM((tm, D), F32),
                        pltpu.VMEM((tm, D), F32)],
        compiler_params=_params(1),
        name="conv_ffn",
    )(x.reshape(B * S, D), w1.astype(BF16), b1.reshape(1, 2 * D), wdw_rep, bdw.reshape(1, D),
      cg.reshape(1, D), cb.reshape(1, D), w2.astype(BF16), b2.reshape(1, D),
      g1.reshape(1, D), bb1.reshape(1, D), wg.astype(BF16), wu.astype(BF16), wd.astype(BF16),
      g2.reshape(1, D), bb2.reshape(1, D))
    return out.reshape(B, S, D)


def kernel(x, attn_w_qkv, attn_w_o, attn_lambda_q1, attn_lambda_k1, attn_lambda_q2, attn_lambda_k2,
           attn_subln_g, conv_w_pw1, conv_b_pw1, conv_w_dw, conv_b_dw, conv_ln_g, conv_ln_b,
           conv_w_pw2, conv_b_pw2, ffn_w_gate, ffn_w_up, ffn_w_down, ln_g, ln_b):
    B, S, D = x.shape
    depth = ln_g.shape[0]
    alpha = (2.0 * depth) ** 0.25
    for i in range(depth):
        j = i // N_MIXERS
        if i % N_MIXERS == 0:
            lambda_init = 0.8 - 0.6 * math.exp(-0.3 * i)
            qt, k, vt = _qkv_rope(x, attn_w_qkv[j])
            lam_vecs = jnp.stack([attn_lambda_q1[j], attn_lambda_k1[j],
                                  attn_lambda_q2[j], attn_lambda_k2[j]]).astype(F32)
            a = _diff_attn(qt, k, vt, lam_vecs, attn_subln_g[j].astype(F32), lambda_init)
            x = _proj_ffn(a.reshape(B * S, D), attn_w_o[j], x.reshape(B * S, D),
                          ln_g[i, 0], ln_b[i, 0], ffn_w_gate[i], ffn_w_up[i], ffn_w_down[i],
                          ln_g[i, 1], ln_b[i, 1], alpha).reshape(B, S, D)
        else:
            x = _conv_ffn(x, conv_w_pw1[j], conv_b_pw1[j], conv_w_dw[j], conv_b_dw[j],
                          conv_ln_g[j], conv_ln_b[j], conv_w_pw2[j], conv_b_pw2[j],
                          ln_g[i, 0], ln_b[i, 0], ffn_w_gate[i], ffn_w_up[i], ffn_w_down[i],
                          ln_g[i, 1], ln_b[i, 1], alpha)
    return x
```

```python
import functools
import math

import jax
import jax.numpy as jnp
from jax import lax
from jax.experimental import pallas as pl
from jax.experimental.pallas import tpu as pltpu

N_HEADS = 8
HEAD_DIM = 64
V_DIM = 2 * HEAD_DIM
V_AUG = V_DIM + 16
CHUNK = 64
ROPE_THETA = 10000.0
CONV_WIDTH = 31
LN_EPS = 1e-5
MASK_VALUE = -1e30
N_MIXERS = 2

V7X_VMEM_BYTES = 64 * 1024 * 1024
VMEM_LIMIT_BYTES = V7X_VMEM_BYTES - 4 * 1024 * 1024
SUBLANES = 8
LANES = 128

ROW_TILE = 512
ATTN_Q_TILE = 256
ATTN_K_TILE = 256
ATTN_HEADS_PER_STEP = 8
HALO_ROWS = 32
CONV_ROWS = 64
GLU_COLS = 1024
FFN_SLICE = 256
LOG2_E = math.log2(math.e)

BF16 = jnp.bfloat16
F32 = jnp.float32


def _params(n_axes):
    return pltpu.CompilerParams(
        dimension_semantics=("arbitrary",) * n_axes,
        vmem_limit_bytes=VMEM_LIMIT_BYTES,
    )


def _resident(shape):
    zeros = (0,) * len(shape)
    return pl.BlockSpec(shape, lambda *_: zeros, pipeline_mode=pl.Buffered(1))


def _layer_norm(z, g, b):
    mu = jnp.mean(z, axis=-1, keepdims=True)
    zc = z - mu
    var = jnp.mean(zc * zc, axis=-1, keepdims=True)
    return zc * lax.rsqrt(var + LN_EPS) * g + b


def _sigmoid(v):
    return 1.0 / (1.0 + jnp.exp(-v))


def _qkv_rope_kernel(x_ref, wqt_ref, wk_ref, wvt_ref, cos_ref, sin_lo_ref, sin_hi_ref,
                     cost_ref, sint_ref, qt_ref, k_ref, vt_ref):
    xb = x_ref[...].astype(BF16)
    nt = (((1,), (1,)), ((), ()))

    k = jnp.dot(xb, wk_ref[...], preferred_element_type=F32)
    cos = cos_ref[...]
    sin_lo = sin_lo_ref[...]
    sin_hi = sin_hi_ref[...]
    half = HEAD_DIM // 2
    for h in range(N_HEADS):
        kh = k[:, h * V_DIM:(h + 1) * V_DIM]
        rot = (kh * cos
               + pltpu.roll(kh, V_DIM - half, 1) * sin_lo
               + pltpu.roll(kh, half, 1) * sin_hi)
        k_ref[:, h * V_DIM:(h + 1) * V_DIM] = rot.astype(BF16)

    qt = lax.dot_general(wqt_ref[...], xb, nt, preferred_element_type=F32)
    cost = cost_ref[...]
    sint = sint_ref[...]
    scale = HEAD_DIM ** -0.5 * LOG2_E
    for g in range(2 * N_HEADS):
        r0 = g * HEAD_DIM
        x1 = qt[r0:r0 + half]
        x2 = qt[r0 + half:r0 + HEAD_DIM]
        qt_ref[r0:r0 + half, :] = ((x1 * cost - x2 * sint) * scale).astype(BF16)
        qt_ref[r0 + half:r0 + HEAD_DIM, :] = ((x2 * cost + x1 * sint) * scale).astype(BF16)

    vt = lax.dot_general(wvt_ref[...], xb, nt, preferred_element_type=F32)
    tm = vt.shape[1]
    ones_rows = (lax.broadcasted_iota(jnp.int32, (V_AUG - V_DIM, tm), 0) == 0).astype(BF16)
    for h in range(N_HEADS):
        vt_ref[h, :V_DIM, :] = vt[h * V_DIM:(h + 1) * V_DIM].astype(BF16)
        vt_ref[h, V_DIM:, :] = ones_rows


def _qkv_rope(x, w_qkv):
    B, S, D = x.shape
    tm = min(ROW_TILE, S)
    half = HEAD_DIM // 2
    wq, wk, wv = jnp.split(w_qkv, 3, axis=-1)
    wqt = wq.T.astype(BF16)
    wvt = wv.T.astype(BF16)
    wk = wk.astype(BF16)

    pos = jnp.arange(S, dtype=F32)
    inv_freq = ROPE_THETA ** (-jnp.arange(0, HEAD_DIM, 2, dtype=F32) / HEAD_DIM)
    ang = pos[:, None] * inv_freq[None, :]
    cos, sin = jnp.cos(ang), jnp.sin(ang)
    reps = V_DIM // half
    cos_l = jnp.tile(cos, (1, reps))
    sin_l = jnp.tile(sin, (1, reps))
    is_x1 = ((jnp.arange(V_DIM) // half) % 2 == 0)[None, :]
    sin_lo = jnp.where(is_x1, -sin_l, 0.0)
    sin_hi = jnp.where(is_x1, 0.0, sin_l)

    row = lambda b, i: (b, i, 0)
    col = lambda b, i: (b, 0, i)
    tab = pl.BlockSpec((tm, V_DIM), lambda b, i: (i, 0))
    tabt = pl.BlockSpec((half, tm), lambda b, i: (0, i))
    return pl.pallas_call(
        _qkv_rope_kernel,
        grid=(B, S // tm),
        in_specs=[pl.BlockSpec((None, tm, D), row),
                  _resident((D, D)), _resident((D, D)), _resident((D, D)),
                  tab, tab, tab, tabt, tabt],
        out_specs=[pl.BlockSpec((None, D, tm), col),
                   pl.BlockSpec((None, tm, D), row),
                   pl.BlockSpec((None, N_HEADS, V_AUG, tm), lambda b, i: (b, 0, 0, i))],
        out_shape=[jax.ShapeDtypeStruct((B, D, S), BF16),
                   jax.ShapeDtypeStruct((B, S, D), BF16),
                   jax.ShapeDtypeStruct((B, N_HEADS, V_AUG, S), BF16)],
        compiler_params=_params(2),
        name="qkv_rope",
    )(x, wqt, wk, wvt, cos_l, sin_lo, sin_hi, cos.T, sin.T)


def _diff_attn_kernel(qt_ref, qt_next_ref, k_ref, vt_ref, lam_ref, g_ref, o_ref,
                      qbd_ref, s_ref, acc_ref, m_ref, *, lambda_init):
    tq = ATTN_Q_TILE
    tk = ATTN_K_TILE
    n_diag = tq // tk
    hps = ATTN_HEADS_PER_STEP
    qi = pl.program_id(2)
    lanes = [slice(g * V_DIM, (g + 1) * V_DIM) for g in range(hps)]

    def load_q(src_ref, g):
        q = src_ref[lanes[g], :]
        qbd_ref[g, :HEAD_DIM, :tq] = q[:HEAD_DIM]
        qbd_ref[g, HEAD_DIM:, tq:] = q[HEAD_DIM:]

    def scores(ki, g):
        k0 = pl.multiple_of(ki * tk, tk)
        s_ref[g] = jnp.dot(k_ref[pl.ds(k0, tk), lanes[g]], qbd_ref[g],
                           preferred_element_type=F32)

    def softmax(g, key_offset):
        s = s_ref[g]
        if key_offset is not None:
            key_chunk = (lax.broadcasted_iota(jnp.int32, s.shape, 0) + key_offset) // CHUNK
            q_chunk = (lax.broadcasted_iota(jnp.int32, s.shape, 1) % tq) // CHUNK
            s = jnp.where(key_chunk <= q_chunk, s, MASK_VALUE)
        s3 = s.reshape(tk // SUBLANES, SUBLANES, 2 * tq)
        m_old = m_ref[g]
        m_tile = jnp.max(s3, axis=0)
        for shift in (4, 2, 1):
            m_tile = jnp.maximum(m_tile, pltpu.roll(m_tile, shift, 0))
        m_new = jnp.maximum(m_old, m_tile)
        alpha = jnp.exp2(m_old - m_new)
        p = jnp.exp2(s3 - m_new).reshape(tk, 2 * tq).astype(BF16)
        m_ref[g] = m_new
        return p, alpha

    def pv_update(ki, g, p, alpha):
        k0 = pl.multiple_of(ki * tk, tk)
        pv = jnp.dot(vt_ref[g, :, pl.ds(k0, tk)], p, preferred_element_type=F32)
        acc3 = acc_ref[g].reshape(V_AUG // SUBLANES, SUBLANES, 2 * tq)
        acc_ref[g] = (alpha * acc3).reshape(V_AUG, 2 * tq) + pv

    def key_tile(ki, key_offset, last):
        for g in range(hps):
            p, alpha = softmax(g, key_offset)
            if last:
                load_q(qt_next_ref, g)
                scores(0, g)
            else:
                scores(ki + 1, g)
            pv_update(ki, g, p, alpha)

    @pl.when(qi == 0)
    def _():
        zero = jnp.zeros((HEAD_DIM, tq), BF16)
        for g in range(hps):
            qbd_ref[g, :HEAD_DIM, tq:] = zero
            qbd_ref[g, HEAD_DIM:, :tq] = zero
            load_q(qt_ref, g)
            scores(0, g)

    m_ref[...] = jnp.full(m_ref.shape, MASK_VALUE, F32)
    acc_ref[...] = jnp.zeros(acc_ref.shape, F32)

    def body(ki, carry):
        key_tile(ki, None, False)
        return carry

    n_full = qi * n_diag
    lax.fori_loop(0, n_full, body, 0)
    for j in range(n_diag):
        key_tile(n_full + j, j * tk, j + 1 == n_diag)

    lq = lam_ref[...]
    lam = (jnp.exp(jnp.sum(lq[0:1] * lq[1:2], axis=1, keepdims=True))
           - jnp.exp(jnp.sum(lq[2:3] * lq[3:4], axis=1, keepdims=True))
           + lambda_init)
    for g in range(hps):
        o = acc_ref[g, :V_DIM, :] / acc_ref[g, V_DIM:V_DIM + 1, :]
        d = o[:, :tq] - lam * o[:, tq:]
        ms = jnp.mean(d * d, axis=0, keepdims=True)
        y = d * lax.rsqrt(ms + LN_EPS) * g_ref[...] * (1.0 - lambda_init)
        o_ref[:, g * V_DIM:(g + 1) * V_DIM] = y.T.astype(BF16)


def _diff_attn(qt, k, vt, lam_vecs, subln_g, lambda_init):
    B, D, S = qt.shape
    tq = min(ATTN_Q_TILE, S)
    tk = ATTN_K_TILE
    hps = ATTN_HEADS_PER_STEP
    w = hps * V_DIM
    assert tq == ATTN_Q_TILE and S % tq == 0 and tq % tk == 0 and tk % CHUNK == 0
    assert N_HEADS % hps == 0
    last_q = S // tq - 1
    return pl.pallas_call(
        functools.partial(_diff_attn_kernel, lambda_init=lambda_init),
        grid=(B, N_HEADS // hps, S // tq),
        in_specs=[pl.BlockSpec((None, w, tq), lambda b, h, i: (b, h, i)),
                  pl.BlockSpec((None, w, tq), lambda b, h, i: (b, h, jnp.minimum(i + 1, last_q))),
                  pl.BlockSpec((None, S, w), lambda b, h, i: (b, 0, h), pipeline_mode=pl.Buffered(1)),
                  pl.BlockSpec((None, hps, V_AUG, S), lambda b, h, i: (b, h, 0, 0),
                               pipeline_mode=pl.Buffered(1)),
                  pl.BlockSpec((4, HEAD_DIM), lambda b, h, i: (0, 0)),
                  pl.BlockSpec((V_DIM, 1), lambda b, h, i: (0, 0))],
        out_specs=pl.BlockSpec((None, tq, w), lambda b, h, i: (b, i, h)),
        out_shape=jax.ShapeDtypeStruct((B, S, D), BF16),
        scratch_shapes=[pltpu.VMEM((hps, V_DIM, 2 * tq), BF16),
                        pltpu.VMEM((hps, tk, 2 * tq), F32),
                        pltpu.VMEM((hps, V_AUG, 2 * tq), F32),
                        pltpu.VMEM((hps, SUBLANES, 2 * tq), F32)],
        compiler_params=_params(3),
        name="diff_attn",
    )(qt, qt, k, vt, lam_vecs, subln_g.reshape(V_DIM, 1))


def _proj_ffn_kernel(a_ref, wo_ref, x_ref, g1_ref, b1_ref, wg_ref, wu_ref, wd_ref, g2_ref, b2_ref,
                     o_ref, *, alpha):
    tm = x_ref.shape[0]
    halves = [slice(r * (tm // 2), (r + 1) * (tm // 2)) for r in range(2)]
    x1s = []
    for rows in halves:
        y = jnp.dot(a_ref[rows, :], wo_ref[...], preferred_element_type=F32)
        x1s.append(_layer_norm(alpha * x_ref[rows, :] + y, g1_ref[...], b1_ref[...]))
    for rows, x1 in zip(halves, x1s):
        xb = x1.astype(BF16)
        gate = jnp.dot(xb, wg_ref[...], preferred_element_type=F32)
        up = jnp.dot(xb, wu_ref[...], preferred_element_type=F32)
        h = (gate * _sigmoid(gate)) * up
        f = jnp.dot(h.astype(BF16), wd_ref[...], preferred_element_type=F32)
        o_ref[rows, :] = _layer_norm(alpha * x1 + f, g2_ref[...], b2_ref[...])


def _proj_ffn(a, wo, x, g1, b1, wg, wu, wd, g2, b2, alpha):
    N, D = x.shape
    tm = min(ROW_TILE, N)
    row = pl.BlockSpec((tm, D), lambda i: (i, 0))
    vec = pl.BlockSpec((1, D), lambda i: (0, 0))
    return pl.pallas_call(
        functools.partial(_proj_ffn_kernel, alpha=alpha),
        grid=(N // tm,),
        in_specs=[pl.BlockSpec((tm, a.shape[1]), lambda i: (i, 0)), _resident(wo.shape), row, vec, vec,
                  _resident(wg.shape), _resident(wu.shape), _resident(wd.shape), vec, vec],
        out_specs=row,
        out_shape=jax.ShapeDtypeStruct((N, D), F32),
        compiler_params=_params(1),
        name="proj_ffn",
    )(a, wo.astype(BF16), x, g1.reshape(1, D), b1.reshape(1, D),
      wg.astype(BF16), wu.astype(BF16), wd.astype(BF16), g2.reshape(1, D), b2.reshape(1, D))


def _exact_zero_from(v):
    bits = pltpu.bitcast(v, jnp.uint32)
    bits = lax.shift_right_logical(lax.shift_right_logical(bits, jnp.uint32(16)), jnp.uint32(16))
    return pltpu.bitcast(bits, F32)


def _ffn_slice(xb, wg_ref, wu_ref, wd_ref, s):
    cols = slice(s * FFN_SLICE, (s + 1) * FFN_SLICE)
    gate = jnp.dot(xb, wg_ref[:, cols], preferred_element_type=F32)
    up = jnp.dot(xb, wu_ref[:, cols], preferred_element_type=F32)
    h = (gate * _sigmoid(gate)) * up
    return jnp.dot(h.astype(BF16), wd_ref[cols, :], preferred_element_type=F32)


def _conv_ffn_kernel(x_ref, w1_ref, b1_ref, wdw_ref, bdw_ref, cg_ref, cb_ref, w2_ref, b2_ref,
                     g1_ref, bb1_ref, wg_ref, wu_ref, wd_ref, g2_ref, bb2_ref, o_ref,
                     hist_ref, conv_ref, mid_ref, *, alpha, tiles_per_seq):
    i = pl.program_id(0)
    tm, D = x_ref.shape
    n_chunks = tm // CONV_ROWS
    n_slices = wg_ref.shape[1] // FFN_SLICE

    @pl.when(i % tiles_per_seq == 0)
    def _():
        hist_ref[:HALO_ROWS, :] = jnp.zeros((HALO_ROWS, D), F32)

    @pl.when(i == 0)
    def _():
        mid_ref[...] = jnp.zeros(mid_ref.shape, F32)

    xm = mid_ref[...]
    xmb = xm.astype(BF16)

    x = x_ref[...]
    xb = x.astype(BF16)

    def glu_block(cb):
        ca = slice(cb * GLU_COLS, (cb + 1) * GLU_COLS)
        cg = slice(D + cb * GLU_COLS, D + (cb + 1) * GLU_COLS)
        ha = jnp.dot(xb, w1_ref[:, ca], preferred_element_type=F32) + b1_ref[:, ca]
        hg = jnp.dot(xb, w1_ref[:, cg], preferred_element_type=F32) + b1_ref[:, cg]
        hist_ref[HALO_ROWS:, ca] = ha * _sigmoid(hg)

    base = HALO_ROWS - (CONV_WIDTH - 1)
    win = CONV_ROWS + HALO_ROWS
    f = jnp.zeros((tm, D), F32)
    acc0 = jnp.zeros((CONV_ROWS, LANES), F32)
    n_cb = D // GLU_COLS
    n_units = n_cb * n_chunks
    for u in range(n_units):
        cb, ci = divmod(u, n_chunks)
        if ci == 0:
            glu_block(cb)
        row0 = ci * CONV_ROWS
        for c in range(cb * GLU_COLS // LANES, (cb + 1) * GLU_COLS // LANES):
            cols = slice(c * LANES, (c + 1) * LANES)
            a = hist_ref[row0:row0 + win, cols]
            acc = acc0
            for r in range(SUBLANES):
                rows = CONV_ROWS if r == 0 else CONV_ROWS + SUBLANES
                part = None
                for o in range(r, base + CONV_WIDTH, SUBLANES):
                    j = o - base
                    if j >= 0:
                        a3 = a[o - r:o - r + rows, :].reshape(rows // SUBLANES, SUBLANES, LANES)
                        term = (a3 * wdw_ref[j, :, cols]).reshape(rows, LANES)
                        part = term if part is None else part + term
                if r > 0:
                    part = pltpu.roll(part, rows - r, 0)[:CONV_ROWS, :]
                acc = acc + part
            conv_ref[row0:row0 + CONV_ROWS, cols] = acc + bdw_ref[:, cols]
        for s in range(u * n_slices // n_units, (u + 1) * n_slices // n_units):
            fs = _ffn_slice(xmb, wg_ref, wu_ref, wd_ref, s)
            f = f + fs
            acc0 = jnp.concatenate([_exact_zero_from(fs[:SUBLANES, :LANES])] * (CONV_ROWS // SUBLANES),
                                   axis=0)

    hist_ref[:HALO_ROWS, :] = hist_ref[tm:tm + HALO_ROWS, :]
    c = _layer_norm(conv_ref[...], cg_ref[...], cb_ref[...])
    c = c * _sigmoid(c)
    y = jnp.dot(c.astype(BF16), w2_ref[...], preferred_element_type=F32) + b2_ref[...]
    mid_ref[...] = _layer_norm(alpha * x + y, g1_ref[...], bb1_ref[...])
    o_ref[...] = _layer_norm(alpha * xm + f, g2_ref[...], bb2_ref[...])


def _conv_ffn(x, w1, b1, wdw, bdw, cg, cb, w2, b2, g1, bb1, wg, wu, wd, g2, bb2, alpha):
    B, S, D = x.shape
    tm = min(ROW_TILE, S)
    assert S % tm == 0 and tm % CONV_ROWS == 0 and HALO_ROWS >= CONV_WIDTH - 1
    assert wg.shape[1] % FFN_SLICE == 0 and D % GLU_COLS == 0
    n_tiles = B * S // tm
    wdw_rep = jnp.broadcast_to(wdw[:, None, :], (CONV_WIDTH, SUBLANES, D))
    vec = lambda n: pl.BlockSpec((1, n), lambda i: (0, 0))
    out = pl.pallas_call(
        functools.partial(_conv_ffn_kernel, alpha=alpha, tiles_per_seq=S // tm),
        grid=(n_tiles + 1,),
        in_specs=[pl.BlockSpec((tm, D), lambda i: (jnp.minimum(i, n_tiles - 1), 0)),
                  _resident(w1.shape), vec(2 * D), _resident(wdw_rep.shape), vec(D), vec(D), vec(D),
                  _resident(w2.shape), vec(D), vec(D), vec(D),
                  _resident(wg.shape), _resident(wu.shape), _resident(wd.shape), vec(D), vec(D)],
        out_specs=pl.BlockSpec((tm, D), lambda i: (jnp.maximum(i - 1, 0), 0)),
        out_shape=jax.ShapeDtypeStruct((B * S, D), F32),
        scratch_shapes=[pltpu.VMEM((tm + HALO_ROWS, D), F32), pltpu.VMEM((tm, D), F32),
                        pltpu.VMEM((tm, D), F32)],
        compiler_params=_params(1),
        name="conv_ffn",
    )(x.reshape(B * S, D), w1.astype(BF16), b1.reshape(1, 2 * D), wdw_rep, bdw.reshape(1, D),
      cg.reshape(1, D), cb.reshape(1, D), w2.astype(BF16), b2.reshape(1, D),
      g1.reshape(1, D), bb1.reshape(1, D), wg.astype(BF16), wu.astype(BF16), wd.astype(BF16),
      g2.reshape(1, D), bb2.reshape(1, D))
    return out.reshape(B, S, D)


def kernel(x, attn_w_qkv, attn_w_o, attn_lambda_q1, attn_lambda_k1, attn_lambda_q2, attn_lambda_k2,
           attn_subln_g, conv_w_pw1, conv_b_pw1, conv_w_dw, conv_b_dw, conv_ln_g, conv_ln_b,
           conv_w_pw2, conv_b_pw2, ffn_w_gate, ffn_w_up, ffn_w_down, ln_g, ln_b):
    B, S, D = x.shape
    depth = ln_g.shape[0]
    alpha = (2.0 * depth) ** 0.25
    for i in range(depth):
        j = i // N_MIXERS
        if i % N_MIXERS == 0:
            lambda_init = 0.8 - 0.6 * math.exp(-0.3 * i)
            qt, k, vt = _qkv_rope(x, attn_w_qkv[j])
            lam_vecs = jnp.stack([attn_lambda_q1[j], attn_lambda_k1[j],
                                  attn_lambda_q2[j], attn_lambda_k2[j]]).astype(F32)
            a = _diff_attn(qt, k, vt, lam_vecs, attn_subln_g[j].astype(F32), lambda_init)
            x = _proj_ffn(a.reshape(B * S, D), attn_w_o[j], x.reshape(B * S, D),
                          ln_g[i, 0], ln_b[i, 0], ffn_w_gate[i], ffn_w_up[i], ffn_w_down[i],
                          ln_g[i, 1], ln_b[i, 1], alpha).reshape(B, S, D)
        else:
            x = _conv_ffn(x, conv_w_pw1[j], conv_b_pw1[j], conv_w_dw[j], conv_b_dw[j],
                          conv_ln_g[j], conv_ln_b[j], conv_w_pw2[j], conv_b_pw2[j],
                          ln_g[i, 0], ln_b[i, 0], ffn_w_gate[i], ffn_w_up[i], ffn_w_down[i],
                          ln_g[i, 1], ln_b[i, 1], alpha)
    return x
```

```python
import functools
import math

import jax
import jax.numpy as jnp
from jax import lax
from jax.experimental import pallas as pl
from jax.experimental.pallas import tpu as pltpu

N_HEADS = 8
HEAD_DIM = 64
V_DIM = 2 * HEAD_DIM
V_AUG = V_DIM + 16
CHUNK = 64
ROPE_THETA = 10000.0
CONV_WIDTH = 31
LN_EPS = 1e-5
MASK_VALUE = -1e30
N_MIXERS = 2

V7X_VMEM_BYTES = 64 * 1024 * 1024
VMEM_LIMIT_BYTES = V7X_VMEM_BYTES - 4 * 1024 * 1024
SUBLANES = 8
LANES = 128

ROW_TILE = 512
ATTN_Q_TILE = 256
ATTN_K_TILE = 256
ATTN_HEADS_PER_STEP = 8
HALO_ROWS = 32
CONV_ROWS = 64
FFN_SLICE = 256
LOG2_E = math.log2(math.e)

BF16 = jnp.bfloat16
F32 = jnp.float32


def _params(n_axes):
    return pltpu.CompilerParams(
        dimension_semantics=("arbitrary",) * n_axes,
        vmem_limit_bytes=VMEM_LIMIT_BYTES,
    )


def _resident(shape):
    zeros = (0,) * len(shape)
    return pl.BlockSpec(shape, lambda *_: zeros, pipeline_mode=pl.Buffered(1))


def _layer_norm(z, g, b):
    mu = jnp.mean(z, axis=-1, keepdims=True)
    zc = z - mu
    var = jnp.mean(zc * zc, axis=-1, keepdims=True)
    return zc * lax.rsqrt(var + LN_EPS) * g + b


def _sigmoid(v):
    return 1.0 / (1.0 + jnp.exp(-v))


def _qkv_rope_kernel(x_ref, wqt_ref, wk_ref, wvt_ref, cos_ref, sin_lo_ref, sin_hi_ref,
                     cost_ref, sint_ref, qt_ref, k_ref, vt_ref):
    xb = x_ref[...].astype(BF16)
    nt = (((1,), (1,)), ((), ()))

    k = jnp.dot(xb, wk_ref[...], preferred_element_type=F32)
    cos = cos_ref[...]
    sin_lo = sin_lo_ref[...]
    sin_hi = sin_hi_ref[...]
    half = HEAD_DIM // 2
    for h in range(N_HEADS):
        kh = k[:, h * V_DIM:(h + 1) * V_DIM]
        rot = (kh * cos
               + pltpu.roll(kh, V_DIM - half, 1) * sin_lo
               + pltpu.roll(kh, half, 1) * sin_hi)
        k_ref[:, h * V_DIM:(h + 1) * V_DIM] = rot.astype(BF16)

    qt = lax.dot_general(wqt_ref[...], xb, nt, preferred_element_type=F32)
    cost = cost_ref[...]
    sint = sint_ref[...]
    scale = HEAD_DIM ** -0.5 * LOG2_E
    for g in range(2 * N_HEADS):
        r0 = g * HEAD_DIM
        x1 = qt[r0:r0 + half]
        x2 = qt[r0 + half:r0 + HEAD_DIM]
        qt_ref[r0:r0 + half, :] = ((x1 * cost - x2 * sint) * scale).astype(BF16)
        qt_ref[r0 + half:r0 + HEAD_DIM, :] = ((x2 * cost + x1 * sint) * scale).astype(BF16)

    vt = lax.dot_general(wvt_ref[...], xb, nt, preferred_element_type=F32)
    tm = vt.shape[1]
    ones_rows = (lax.broadcasted_iota(jnp.int32, (V_AUG - V_DIM, tm), 0) == 0).astype(BF16)
    for h in range(N_HEADS):
        vt_ref[h, :V_DIM, :] = vt[h * V_DIM:(h + 1) * V_DIM].astype(BF16)
        vt_ref[h, V_DIM:, :] = ones_rows


def _qkv_rope(x, w_qkv):
    B, S, D = x.shape
    tm = min(ROW_TILE, S)
    half = HEAD_DIM // 2
    wq, wk, wv = jnp.split(w_qkv, 3, axis=-1)
    wqt = wq.T.astype(BF16)
    wvt = wv.T.astype(BF16)
    wk = wk.astype(BF16)

    pos = jnp.arange(S, dtype=F32)
    inv_freq = ROPE_THETA ** (-jnp.arange(0, HEAD_DIM, 2, dtype=F32) / HEAD_DIM)
    ang = pos[:, None] * inv_freq[None, :]
    cos, sin = jnp.cos(ang), jnp.sin(ang)
    reps = V_DIM // half
    cos_l = jnp.tile(cos, (1, reps))
    sin_l = jnp.tile(sin, (1, reps))
    is_x1 = ((jnp.arange(V_DIM) // half) % 2 == 0)[None, :]
    sin_lo = jnp.where(is_x1, -sin_l, 0.0)
    sin_hi = jnp.where(is_x1, 0.0, sin_l)

    row = lambda b, i: (b, i, 0)
    col = lambda b, i: (b, 0, i)
    tab = pl.BlockSpec((tm, V_DIM), lambda b, i: (i, 0))
    tabt = pl.BlockSpec((half, tm), lambda b, i: (0, i))
    return pl.pallas_call(
        _qkv_rope_kernel,
        grid=(B, S // tm),
        in_specs=[pl.BlockSpec((None, tm, D), row),
                  _resident((D, D)), _resident((D, D)), _resident((D, D)),
                  tab, tab, tab, tabt, tabt],
        out_specs=[pl.BlockSpec((None, D, tm), col),
                   pl.BlockSpec((None, tm, D), row),
                   pl.BlockSpec((None, N_HEADS, V_AUG, tm), lambda b, i: (b, 0, 0, i))],
        out_shape=[jax.ShapeDtypeStruct((B, D, S), BF16),
                   jax.ShapeDtypeStruct((B, S, D), BF16),
                   jax.ShapeDtypeStruct((B, N_HEADS, V_AUG, S), BF16)],
        compiler_params=_params(2),
        name="qkv_rope",
    )(x, wqt, wk, wvt, cos_l, sin_lo, sin_hi, cos.T, sin.T)


def _diff_attn_kernel(qt_ref, qt_next_ref, k_ref, vt_ref, lam_ref, g_ref, o_ref,
                      qbd_ref, s_ref, acc_ref, m_ref, *, lambda_init):
    tq = ATTN_Q_TILE
    tk = ATTN_K_TILE
    n_diag = tq // tk
    hps = ATTN_HEADS_PER_STEP
    qi = pl.program_id(2)
    lanes = [slice(g * V_DIM, (g + 1) * V_DIM) for g in range(hps)]

    def load_q(src_ref, g):
        q = src_ref[lanes[g], :]
        qbd_ref[g, :HEAD_DIM, :tq] = q[:HEAD_DIM]
        qbd_ref[g, HEAD_DIM:, tq:] = q[HEAD_DIM:]

    def scores(ki, g):
        k0 = pl.multiple_of(ki * tk, tk)
        s_ref[g] = jnp.dot(k_ref[pl.ds(k0, tk), lanes[g]], qbd_ref[g],
                           preferred_element_type=F32)

    def softmax(g, key_offset):
        s = s_ref[g]
        if key_offset is not None:
            key_chunk = (lax.broadcasted_iota(jnp.int32, s.shape, 0) + key_offset) // CHUNK
            q_chunk = (lax.broadcasted_iota(jnp.int32, s.shape, 1) % tq) // CHUNK
            s = jnp.where(key_chunk <= q_chunk, s, MASK_VALUE)
        s3 = s.reshape(tk // SUBLANES, SUBLANES, 2 * tq)
        m_old = m_ref[g]
        m_tile = jnp.max(s3, axis=0)
        for shift in (4, 2, 1):
            m_tile = jnp.maximum(m_tile, pltpu.roll(m_tile, shift, 0))
        m_new = jnp.maximum(m_old, m_tile)
        alpha = jnp.exp2(m_old - m_new)
        p = jnp.exp2(s3 - m_new).reshape(tk, 2 * tq).astype(BF16)
        m_ref[g] = m_new
        return p, alpha

    def pv_update(ki, g, p, alpha):
        k0 = pl.multiple_of(ki * tk, tk)
        pv = jnp.dot(vt_ref[g, :, pl.ds(k0, tk)], p, preferred_element_type=F32)
        acc3 = acc_ref[g].reshape(V_AUG // SUBLANES, SUBLANES, 2 * tq)
        acc_ref[g] = (alpha * acc3).reshape(V_AUG, 2 * tq) + pv

    def key_tile(ki, key_offset, last):
        for g in range(hps):
            p, alpha = softmax(g, key_offset)
            if last:
                load_q(qt_next_ref, g)
                scores(0, g)
            else:
                scores(ki + 1, g)
            pv_update(ki, g, p, alpha)

    @pl.when(qi == 0)
    def _():
        zero = jnp.zeros((HEAD_DIM, tq), BF16)
        for g in range(hps):
            qbd_ref[g, :HEAD_DIM, tq:] = zero
            qbd_ref[g, HEAD_DIM:, :tq] = zero
            load_q(qt_ref, g)
            scores(0, g)

    m_ref[...] = jnp.full(m_ref.shape, MASK_VALUE, F32)
    acc_ref[...] = jnp.zeros(acc_ref.shape, F32)

    def body(ki, carry):
        key_tile(ki, None, False)
        return carry

    n_full = qi * n_diag
    lax.fori_loop(0, n_full, body, 0)
    for j in range(n_diag):
        key_tile(n_full + j, j * tk, j + 1 == n_diag)

    lq = lam_ref[...]
    lam = (jnp.exp(jnp.sum(lq[0:1] * lq[1:2], axis=1, keepdims=True))
           - jnp.exp(jnp.sum(lq[2:3] * lq[3:4], axis=1, keepdims=True))
           + lambda_init)
    for g in range(hps):
        o = acc_ref[g, :V_DIM, :] / acc_ref[g, V_DIM:V_DIM + 1, :]
        d = o[:, :tq] - lam * o[:, tq:]
        ms = jnp.mean(d * d, axis=0, keepdims=True)
        y = d * lax.rsqrt(ms + LN_EPS) * g_ref[...] * (1.0 - lambda_init)
        o_ref[:, g * V_DIM:(g + 1) * V_DIM] = y.T.astype(BF16)


def _diff_attn(qt, k, vt, lam_vecs, subln_g, lambda_init):
    B, D, S = qt.shape
    tq = min(ATTN_Q_TILE, S)
    tk = ATTN_K_TILE
    hps = ATTN_HEADS_PER_STEP
    w = hps * V_DIM
    assert tq == ATTN_Q_TILE and S % tq == 0 and tq % tk == 0 and tk % CHUNK == 0
    assert N_HEADS % hps == 0
    last_q = S // tq - 1
    return pl.pallas_call(
        functools.partial(_diff_attn_kernel, lambda_init=lambda_init),
        grid=(B, N_HEADS // hps, S // tq),
        in_specs=[pl.BlockSpec((None, w, tq), lambda b, h, i: (b, h, i)),
                  pl.BlockSpec((None, w, tq), lambda b, h, i: (b, h, jnp.minimum(i + 1, last_q))),
                  pl.BlockSpec((None, S, w), lambda b, h, i: (b, 0, h), pipeline_mode=pl.Buffered(1)),
                  pl.BlockSpec((None, hps, V_AUG, S), lambda b, h, i: (b, h, 0, 0),
                               pipeline_mode=pl.Buffered(1)),
                  pl.BlockSpec((4, HEAD_DIM), lambda b, h, i: (0, 0)),
                  pl.BlockSpec((V_DIM, 1), lambda b, h, i: (0, 0))],
        out_specs=pl.BlockSpec((None, tq, w), lambda b, h, i: (b, i, h)),
        out_shape=jax.ShapeDtypeStruct((B, S, D), BF16),
        scratch_shapes=[pltpu.VMEM((hps, V_DIM, 2 * tq), BF16),
                        pltpu.VMEM((hps, tk, 2 * tq), F32),
                        pltpu.VMEM((hps, V_AUG, 2 * tq), F32),
                        pltpu.VMEM((hps, SUBLANES, 2 * tq), F32)],
        compiler_params=_params(3),
        name="diff_attn",
    )(qt, qt, k, vt, lam_vecs, subln_g.reshape(V_DIM, 1))


def _proj_ffn_kernel(a_ref, wo_ref, x_ref, g1_ref, b1_ref, wg_ref, wu_ref, wd_ref, g2_ref, b2_ref,
                     o_ref, *, alpha):
    tm = x_ref.shape[0]
    halves = [slice(r * (tm // 2), (r + 1) * (tm // 2)) for r in range(2)]
    x1s = []
    for rows in halves:
        y = jnp.dot(a_ref[rows, :], wo_ref[...], preferred_element_type=F32)
        x1s.append(_layer_norm(alpha * x_ref[rows, :] + y, g1_ref[...], b1_ref[...]))
    for rows, x1 in zip(halves, x1s):
        xb = x1.astype(BF16)
        gate = jnp.dot(xb, wg_ref[...], preferred_element_type=F32)
        up = jnp.dot(xb, wu_ref[...], preferred_element_type=F32)
        h = (gate * _sigmoid(gate)) * up
        f = jnp.dot(h.astype(BF16), wd_ref[...], preferred_element_type=F32)
        o_ref[rows, :] = _layer_norm(alpha * x1 + f, g2_ref[...], b2_ref[...])


def _proj_ffn(a, wo, x, g1, b1, wg, wu, wd, g2, b2, alpha):
    N, D = x.shape
    tm = min(ROW_TILE, N)
    row = pl.BlockSpec((tm, D), lambda i: (i, 0))
    vec = pl.BlockSpec((1, D), lambda i: (0, 0))
    return pl.pallas_call(
        functools.partial(_proj_ffn_kernel, alpha=alpha),
        grid=(N // tm,),
        in_specs=[pl.BlockSpec((tm, a.shape[1]), lambda i: (i, 0)), _resident(wo.shape), row, vec, vec,
                  _resident(wg.shape), _resident(wu.shape), _resident(wd.shape), vec, vec],
        out_specs=row,
        out_shape=jax.ShapeDtypeStruct((N, D), F32),
        compiler_params=_params(1),
        name="proj_ffn",
    )(a, wo.astype(BF16), x, g1.reshape(1, D), b1.reshape(1, D),
      wg.astype(BF16), wu.astype(BF16), wd.astype(BF16), g2.reshape(1, D), b2.reshape(1, D))


def _exact_zero_from(v):
    bits = pltpu.bitcast(v, jnp.uint32)
    bits = lax.shift_right_logical(lax.shift_right_logical(bits, jnp.uint32(16)), jnp.uint32(16))
    return pltpu.bitcast(bits, F32)


def _ffn_slice(xb, wg_ref, wu_ref, wd_ref, s):
    cols = slice(s * FFN_SLICE, (s + 1) * FFN_SLICE)
    gate = jnp.dot(xb, wg_ref[:, cols], preferred_element_type=F32)
    up = jnp.dot(xb, wu_ref[:, cols], preferred_element_type=F32)
    h = (gate * _sigmoid(gate)) * up
    return jnp.dot(h.astype(BF16), wd_ref[cols, :], preferred_element_type=F32)


def _conv_ffn_kernel(x_ref, w1_ref, b1_ref, wdw_ref, bdw_ref, cg_ref, cb_ref, w2_ref, b2_ref,
                     g1_ref, bb1_ref, wg_ref, wu_ref, wd_ref, g2_ref, bb2_ref, o_ref,
                     hist_ref, conv_ref, mid_ref, *, alpha, tiles_per_seq):
    i = pl.program_id(0)
    tm, D = x_ref.shape
    n_chunks = tm // CONV_ROWS
    n_slices = wg_ref.shape[1] // FFN_SLICE

    @pl.when(i % tiles_per_seq == 0)
    def _():
        hist_ref[:HALO_ROWS, :] = jnp.zeros((HALO_ROWS, D), F32)

    @pl.when(i == 0)
    def _():
        mid_ref[...] = jnp.zeros(mid_ref.shape, F32)

    xm = mid_ref[...]
    xmb = xm.astype(BF16)

    x = x_ref[...]
    xb = x.astype(BF16)
    ha = jnp.dot(xb, w1_ref[:, :D], preferred_element_type=F32) + b1_ref[:, :D]
    hg = jnp.dot(xb, w1_ref[:, D:], preferred_element_type=F32) + b1_ref[:, D:]
    hist_ref[HALO_ROWS:, :] = ha * _sigmoid(hg)

    base = HALO_ROWS - (CONV_WIDTH - 1)
    win = CONV_ROWS + HALO_ROWS
    f = jnp.zeros((tm, D), F32)
    acc0 = jnp.zeros((CONV_ROWS, LANES), F32)
    for ci in range(n_chunks):
        row0 = ci * CONV_ROWS
        acc_init = acc0
        for s in range(ci * n_slices // n_chunks, (ci + 1) * n_slices // n_chunks):
            fs = _ffn_slice(xmb, wg_ref, wu_ref, wd_ref, s)
            f = f + fs
            acc0 = jnp.concatenate([_exact_zero_from(fs[:SUBLANES, :LANES])] * (CONV_ROWS // SUBLANES),
                                   axis=0)
        for c in range(D // LANES):
            cols = slice(c * LANES, (c + 1) * LANES)
            a = hist_ref[row0:row0 + win, cols]
            acc = acc_init
            for r in range(SUBLANES):
                rows = CONV_ROWS if r == 0 else CONV_ROWS + SUBLANES
                part = None
                for o in range(r, base + CONV_WIDTH, SUBLANES):
                    j = o - base
                    if j >= 0:
                        a3 = a[o - r:o - r + rows, :].reshape(rows // SUBLANES, SUBLANES, LANES)
                        term = (a3 * wdw_ref[j, :, cols]).reshape(rows, LANES)
                        part = term if part is None else part + term
                if r > 0:
                    part = pltpu.roll(part, rows - r, 0)[:CONV_ROWS, :]
                acc = acc + part
            conv_ref[row0:row0 + CONV_ROWS, cols] = acc + bdw_ref[:, cols]

    hist_ref[:HALO_ROWS, :] = hist_ref[tm:tm + HALO_ROWS, :]
    c = _layer_norm(conv_ref[...], cg_ref[...], cb_ref[...])
    c = c * _sigmoid(c)
    y = jnp.dot(c.astype(BF16), w2_ref[...], preferred_element_type=F32) + b2_ref[...]
    mid_ref[...] = _layer_norm(alpha * x + y, g1_ref[...], bb1_ref[...])
    o_ref[...] = _layer_norm(alpha * xm + f, g2_ref[...], bb2_ref[...])


def _conv_ffn(x, w1, b1, wdw, bdw, cg, cb, w2, b2, g1, bb1, wg, wu, wd, g2, bb2, alpha):
    B, S, D = x.shape
    tm = min(ROW_TILE, S)
    assert S % tm == 0 and tm % CONV_ROWS == 0 and HALO_ROWS >= CONV_WIDTH - 1
    assert wg.shape[1] % FFN_SLICE == 0
    n_tiles = B * S // tm
    wdw_rep = jnp.broadcast_to(wdw[:, None, :], (CONV_WIDTH, SUBLANES, D))
    vec = lambda n: pl.BlockSpec((1, n), lambda i: (0, 0))
    out = pl.pallas_call(
        functools.partial(_conv_ffn_kernel, alpha=alpha, tiles_per_seq=S // tm),
        grid=(n_tiles + 1,),
        in_specs=[pl.BlockSpec((tm, D), lambda i: (jnp.minimum(i, n_tiles - 1), 0)),
                  _resident(w1.shape), vec(2 * D), _resident(wdw_rep.shape), vec(D), vec(D), vec(D),
                  _resident(w2.shape), vec(D), vec(D), vec(D),
                  _resident(wg.shape), _resident(wu.shape), _resident(wd.shape), vec(D), vec(D)],
        out_specs=pl.BlockSpec((tm, D), lambda i: (jnp.maximum(i - 1, 0), 0)),
        out_shape=jax.ShapeDtypeStruct((B * S, D), F32),
        scratch_shapes=[pltpu.VMEM((tm + HALO_ROWS, D), F32), pltpu.VMEM((tm, D), F32),
                        pltpu.VMEM((tm, D), F32)],
        compiler_params=_params(1),
        name="conv_ffn",
    )(x.reshape(B * S, D), w1.astype(BF16), b1.reshape(1, 2 * D), wdw_rep, bdw.reshape(1, D),
      cg.reshape(1, D), cb.reshape(1, D), w2.astype(BF16), b2.reshape(1, D),
      g1.reshape(1, D), bb1.reshape(1, D), wg.astype(BF16), wu.astype(BF16), wd.astype(BF16),
      g2.reshape(1, D), bb2.reshape(1, D))
    return out.reshape(B, S, D)


def kernel(x, attn_w_qkv, attn_w_o, attn_lambda_q1, attn_lambda_k1, attn_lambda_q2, attn_lambda_k2,
           attn_subln_g, conv_w_pw1, conv_b_pw1, conv_w_dw, conv_b_dw, conv_ln_g, conv_ln_b,
           conv_w_pw2, conv_b_pw2, ffn_w_gate, ffn_w_up, ffn_w_down, ln_g, ln_b):
    B, S, D = x.shape
    depth = ln_g.shape[0]
    alpha = (2.0 * depth) ** 0.25
    for i in range(depth):
        j = i // N_MIXERS
        if i % N_MIXERS == 0:
            lambda_init = 0.8 - 0.6 * math.exp(-0.3 * i)
            qt, k, vt = _qkv_rope(x, attn_w_qkv[j])
            lam_vecs = jnp.stack([attn_lambda_q1[j], attn_lambda_k1[j],
                                  attn_lambda_q2[j], attn_lambda_k2[j]]).astype(F32)
            a = _diff_attn(qt, k, vt, lam_vecs, attn_subln_g[j].astype(F32), lambda_init)
            x = _proj_ffn(a.reshape(B * S, D), attn_w_o[j], x.reshape(B * S, D),
                          ln_g[i, 0], ln_b[i, 0], ffn_w_gate[i], ffn_w_up[i], ffn_w_down[i],
                          ln_g[i, 1], ln_b[i, 1], alpha).reshape(B, S, D)
        else:
            x = _conv_ffn(x, conv_w_pw1[j], conv_b_pw1[j], conv_w_dw[j], conv_b_dw[j],
                          conv_ln_g[j], conv_ln_b[j], conv_w_pw2[j], conv_b_pw2[j],
                          ln_g[i, 0], ln_b[i, 0], ffn_w_gate[i], ffn_w_up[i], ffn_w_down[i],
                          ln_g[i, 1], ln_b[i, 1], alpha)
    return x
```

```python
import functools
import math

import jax
import jax.numpy as jnp
from jax import lax
from jax.experimental import pallas as pl
from jax.experimental.pallas import tpu as pltpu

N_HEADS = 8
HEAD_DIM = 64
V_DIM = 2 * HEAD_DIM
V_AUG = V_DIM + 16
CHUNK = 64
ROPE_THETA = 10000.0
CONV_WIDTH = 31
LN_EPS = 1e-5
MASK_VALUE = -1e30
N_MIXERS = 2

V7X_VMEM_BYTES = 64 * 1024 * 1024
VMEM_LIMIT_BYTES = V7X_VMEM_BYTES - 4 * 1024 * 1024
SUBLANES = 8
LANES = 128

ROW_TILE = 512
ATTN_Q_TILE = 256
ATTN_K_TILE = 256
ATTN_HEADS_PER_STEP = 8
HALO_ROWS = 32
CONV_ROWS = 64
FFN_SLICE = 256
LOG2_E = math.log2(math.e)

BF16 = jnp.bfloat16
F32 = jnp.float32


def _params(n_axes):
    return pltpu.CompilerParams(
        dimension_semantics=("arbitrary",) * n_axes,
        vmem_limit_bytes=VMEM_LIMIT_BYTES,
    )


def _resident(shape):
    zeros = (0,) * len(shape)
    return pl.BlockSpec(shape, lambda *_: zeros, pipeline_mode=pl.Buffered(1))


def _layer_norm(z, g, b):
    mu = jnp.mean(z, axis=-1, keepdims=True)
    zc = z - mu
    var = jnp.mean(zc * zc, axis=-1, keepdims=True)
    return zc * lax.rsqrt(var + LN_EPS) * g + b


def _sigmoid(v):
    return 1.0 / (1.0 + jnp.exp(-v))


def _qkv_rope_kernel(x_ref, wqt_ref, wk_ref, wvt_ref, cos_ref, sin_lo_ref, sin_hi_ref,
                     cost_ref, sint_ref, qt_ref, k_ref, vt_ref):
    xb = x_ref[...].astype(BF16)
    nt = (((1,), (1,)), ((), ()))

    k = jnp.dot(xb, wk_ref[...], preferred_element_type=F32)
    cos = cos_ref[...]
    sin_lo = sin_lo_ref[...]
    sin_hi = sin_hi_ref[...]
    half = HEAD_DIM // 2
    for h in range(N_HEADS):
        kh = k[:, h * V_DIM:(h + 1) * V_DIM]
        rot = (kh * cos
               + pltpu.roll(kh, V_DIM - half, 1) * sin_lo
               + pltpu.roll(kh, half, 1) * sin_hi)
        k_ref[:, h * V_DIM:(h + 1) * V_DIM] = rot.astype(BF16)

    qt = lax.dot_general(wqt_ref[...], xb, nt, preferred_element_type=F32)
    cost = cost_ref[...]
    sint = sint_ref[...]
    scale = HEAD_DIM ** -0.5 * LOG2_E
    for g in range(2 * N_HEADS):
        r0 = g * HEAD_DIM
        x1 = qt[r0:r0 + half]
        x2 = qt[r0 + half:r0 + HEAD_DIM]
        qt_ref[r0:r0 + half, :] = ((x1 * cost - x2 * sint) * scale).astype(BF16)
        qt_ref[r0 + half:r0 + HEAD_DIM, :] = ((x2 * cost + x1 * sint) * scale).astype(BF16)

    vt = lax.dot_general(wvt_ref[...], xb, nt, preferred_element_type=F32)
    tm = vt.shape[1]
    ones_rows = (lax.broadcasted_iota(jnp.int32, (V_AUG - V_DIM, tm), 0) == 0).astype(BF16)
    for h in range(N_HEADS):
        vt_ref[h, :V_DIM, :] = vt[h * V_DIM:(h + 1) * V_DIM].astype(BF16)
        vt_ref[h, V_DIM:, :] = ones_rows


def _qkv_rope(x, w_qkv):
    B, S, D = x.shape
    tm = min(ROW_TILE, S)
    half = HEAD_DIM // 2
    wq, wk, wv = jnp.split(w_qkv, 3, axis=-1)
    wqt = wq.T.astype(BF16)
    wvt = wv.T.astype(BF16)
    wk = wk.astype(BF16)

    pos = jnp.arange(S, dtype=F32)
    inv_freq = ROPE_THETA ** (-jnp.arange(0, HEAD_DIM, 2, dtype=F32) / HEAD_DIM)
    ang = pos[:, None] * inv_freq[None, :]
    cos, sin = jnp.cos(ang), jnp.sin(ang)
    reps = V_DIM // half
    cos_l = jnp.tile(cos, (1, reps))
    sin_l = jnp.tile(sin, (1, reps))
    is_x1 = ((jnp.arange(V_DIM) // half) % 2 == 0)[None, :]
    sin_lo = jnp.where(is_x1, -sin_l, 0.0)
    sin_hi = jnp.where(is_x1, 0.0, sin_l)

    row = lambda b, i: (b, i, 0)
    col = lambda b, i: (b, 0, i)
    tab = pl.BlockSpec((tm, V_DIM), lambda b, i: (i, 0))
    tabt = pl.BlockSpec((half, tm), lambda b, i: (0, i))
    return pl.pallas_call(
        _qkv_rope_kernel,
        grid=(B, S // tm),
        in_specs=[pl.BlockSpec((None, tm, D), row),
                  _resident((D, D)), _resident((D, D)), _resident((D, D)),
                  tab, tab, tab, tabt, tabt],
        out_specs=[pl.BlockSpec((None, D, tm), col),
                   pl.BlockSpec((None, tm, D), row),
                   pl.BlockSpec((None, N_HEADS, V_AUG, tm), lambda b, i: (b, 0, 0, i))],
        out_shape=[jax.ShapeDtypeStruct((B, D, S), BF16),
                   jax.ShapeDtypeStruct((B, S, D), BF16),
                   jax.ShapeDtypeStruct((B, N_HEADS, V_AUG, S), BF16)],
        compiler_params=_params(2),
        name="qkv_rope",
    )(x, wqt, wk, wvt, cos_l, sin_lo, sin_hi, cos.T, sin.T)


def _diff_attn_kernel(qt_ref, qt_next_ref, k_ref, vt_ref, lam_ref, g_ref, o_ref,
                      qbd_ref, s_ref, acc_ref, m_ref, *, lambda_init):
    tq = ATTN_Q_TILE
    tk = ATTN_K_TILE
    n_diag = tq // tk
    hps = ATTN_HEADS_PER_STEP
    qi = pl.program_id(2)
    lanes = [slice(g * V_DIM, (g + 1) * V_DIM) for g in range(hps)]

    def load_q(src_ref, g):
        q = src_ref[lanes[g], :]
        qbd_ref[g, :HEAD_DIM, :tq] = q[:HEAD_DIM]
        qbd_ref[g, HEAD_DIM:, tq:] = q[HEAD_DIM:]

    def scores(ki, g):
        k0 = pl.multiple_of(ki * tk, tk)
        s_ref[g] = jnp.dot(k_ref[pl.ds(k0, tk), lanes[g]], qbd_ref[g],
                           preferred_element_type=F32)

    def softmax(g, key_offset):
        ps, alphas = [], []
        for half in range(2):
            cols = slice(half * tq, (half + 1) * tq)
            s = s_ref[g, :, cols]
            if key_offset is not None:
                key_chunk = (lax.broadcasted_iota(jnp.int32, s.shape, 0) + key_offset) // CHUNK
                q_chunk = lax.broadcasted_iota(jnp.int32, s.shape, 1) // CHUNK
                s = jnp.where(key_chunk <= q_chunk, s, MASK_VALUE)
            s3 = s.reshape(tk // SUBLANES, SUBLANES, tq)
            m_old = m_ref[g, :, cols]
            m_tile = jnp.max(s3, axis=0)
            for shift in (4, 2, 1):
                m_tile = jnp.maximum(m_tile, pltpu.roll(m_tile, shift, 0))
            m_new = jnp.maximum(m_old, m_tile)
            alphas.append(jnp.exp2(m_old - m_new))
            ps.append(jnp.exp2(s3 - m_new).reshape(tk, tq).astype(BF16))
            m_ref[g, :, cols] = m_new
        return jnp.concatenate(ps, axis=1), jnp.concatenate(alphas, axis=1)

    def pv_update(ki, g, p, alpha):
        k0 = pl.multiple_of(ki * tk, tk)
        pv = jnp.dot(vt_ref[g, :, pl.ds(k0, tk)], p, preferred_element_type=F32)
        acc3 = acc_ref[g].reshape(V_AUG // SUBLANES, SUBLANES, 2 * tq)
        acc_ref[g] = (alpha * acc3).reshape(V_AUG, 2 * tq) + pv

    def key_tile(ki, key_offset, last):
        for g in range(hps):
            p, alpha = softmax(g, key_offset)
            if last:
                load_q(qt_next_ref, g)
                scores(0, g)
            else:
                scores(ki + 1, g)
            pv_update(ki, g, p, alpha)

    @pl.when(qi == 0)
    def _():
        zero = jnp.zeros((HEAD_DIM, tq), BF16)
        for g in range(hps):
            qbd_ref[g, :HEAD_DIM, tq:] = zero
            qbd_ref[g, HEAD_DIM:, :tq] = zero
            load_q(qt_ref, g)
            scores(0, g)

    m_ref[...] = jnp.full(m_ref.shape, MASK_VALUE, F32)
    acc_ref[...] = jnp.zeros(acc_ref.shape, F32)

    def body(ki, carry):
        key_tile(ki, None, False)
        return carry

    n_full = qi * n_diag
    lax.fori_loop(0, n_full, body, 0)
    for j in range(n_diag):
        key_tile(n_full + j, j * tk, j + 1 == n_diag)

    lq = lam_ref[...]
    lam = (jnp.exp(jnp.sum(lq[0:1] * lq[1:2], axis=1, keepdims=True))
           - jnp.exp(jnp.sum(lq[2:3] * lq[3:4], axis=1, keepdims=True))
           + lambda_init)
    for g in range(hps):
        o = acc_ref[g, :V_DIM, :] / acc_ref[g, V_DIM:V_DIM + 1, :]
        d = o[:, :tq] - lam * o[:, tq:]
        ms = jnp.mean(d * d, axis=0, keepdims=True)
        y = d * lax.rsqrt(ms + LN_EPS) * g_ref[...] * (1.0 - lambda_init)
        o_ref[:, g * V_DIM:(g + 1) * V_DIM] = y.T.astype(BF16)


def _diff_attn(qt, k, vt, lam_vecs, subln_g, lambda_init):
    B, D, S = qt.shape
    tq = min(ATTN_Q_TILE, S)
    tk = ATTN_K_TILE
    hps = ATTN_HEADS_PER_STEP
    w = hps * V_DIM
    assert tq == ATTN_Q_TILE and S % tq == 0 and tq % tk == 0 and tk % CHUNK == 0
    assert N_HEADS % hps == 0
    last_q = S // tq - 1
    return pl.pallas_call(
        functools.partial(_diff_attn_kernel, lambda_init=lambda_init),
        grid=(B, N_HEADS // hps, S // tq),
        in_specs=[pl.BlockSpec((None, w, tq), lambda b, h, i: (b, h, i)),
                  pl.BlockSpec((None, w, tq), lambda b, h, i: (b, h, jnp.minimum(i + 1, last_q))),
                  pl.BlockSpec((None, S, w), lambda b, h, i: (b, 0, h), pipeline_mode=pl.Buffered(1)),
                  pl.BlockSpec((None, hps, V_AUG, S), lambda b, h, i: (b, h, 0, 0),
                               pipeline_mode=pl.Buffered(1)),
                  pl.BlockSpec((4, HEAD_DIM), lambda b, h, i: (0, 0)),
                  pl.BlockSpec((V_DIM, 1), lambda b, h, i: (0, 0))],
        out_specs=pl.BlockSpec((None, tq, w), lambda b, h, i: (b, i, h)),
        out_shape=jax.ShapeDtypeStruct((B, S, D), BF16),
        scratch_shapes=[pltpu.VMEM((hps, V_DIM, 2 * tq), BF16),
                        pltpu.VMEM((hps, tk, 2 * tq), F32),
                        pltpu.VMEM((hps, V_AUG, 2 * tq), F32),
                        pltpu.VMEM((hps, SUBLANES, 2 * tq), F32)],
        compiler_params=_params(3),
        name="diff_attn",
    )(qt, qt, k, vt, lam_vecs, subln_g.reshape(V_DIM, 1))


def _proj_ffn_kernel(a_ref, wo_ref, x_ref, g1_ref, b1_ref, wg_ref, wu_ref, wd_ref, g2_ref, b2_ref,
                     o_ref, *, alpha):
    tm = x_ref.shape[0]
    halves = [slice(r * (tm // 2), (r + 1) * (tm // 2)) for r in range(2)]
    x1s = []
    for rows in halves:
        y = jnp.dot(a_ref[rows, :], wo_ref[...], preferred_element_type=F32)
        x1s.append(_layer_norm(alpha * x_ref[rows, :] + y, g1_ref[...], b1_ref[...]))
    for rows, x1 in zip(halves, x1s):
        xb = x1.astype(BF16)
        gate = jnp.dot(xb, wg_ref[...], preferred_element_type=F32)
        up = jnp.dot(xb, wu_ref[...], preferred_element_type=F32)
        h = (gate * _sigmoid(gate)) * up
        f = jnp.dot(h.astype(BF16), wd_ref[...], preferred_element_type=F32)
        o_ref[rows, :] = _layer_norm(alpha * x1 + f, g2_ref[...], b2_ref[...])


def _proj_ffn(a, wo, x, g1, b1, wg, wu, wd, g2, b2, alpha):
    N, D = x.shape
    tm = min(ROW_TILE, N)
    row = pl.BlockSpec((tm, D), lambda i: (i, 0))
    vec = pl.BlockSpec((1, D), lambda i: (0, 0))
    return pl.pallas_call(
        functools.partial(_proj_ffn_kernel, alpha=alpha),
        grid=(N // tm,),
        in_specs=[pl.BlockSpec((tm, a.shape[1]), lambda i: (i, 0)), _resident(wo.shape), row, vec, vec,
                  _resident(wg.shape), _resident(wu.shape), _resident(wd.shape), vec, vec],
        out_specs=row,
        out_shape=jax.ShapeDtypeStruct((N, D), F32),
        compiler_params=_params(1),
        name="proj_ffn",
    )(a, wo.astype(BF16), x, g1.reshape(1, D), b1.reshape(1, D),
      wg.astype(BF16), wu.astype(BF16), wd.astype(BF16), g2.reshape(1, D), b2.reshape(1, D))


def _exact_zero_from(v):
    bits = pltpu.bitcast(v, jnp.uint32)
    bits = lax.shift_right_logical(lax.shift_right_logical(bits, jnp.uint32(16)), jnp.uint32(16))
    return pltpu.bitcast(bits, F32)


def _ffn_slice(xb, wg_ref, wu_ref, wd_ref, s):
    cols = slice(s * FFN_SLICE, (s + 1) * FFN_SLICE)
    gate = jnp.dot(xb, wg_ref[:, cols], preferred_element_type=F32)
    up = jnp.dot(xb, wu_ref[:, cols], preferred_element_type=F32)
    h = (gate * _sigmoid(gate)) * up
    return jnp.dot(h.astype(BF16), wd_ref[cols, :], preferred_element_type=F32)


def _conv_ffn_kernel(x_ref, w1_ref, b1_ref, wdw_ref, bdw_ref, cg_ref, cb_ref, w2_ref, b2_ref,
                     g1_ref, bb1_ref, wg_ref, wu_ref, wd_ref, g2_ref, bb2_ref, o_ref,
                     hist_ref, conv_ref, mid_ref, *, alpha, tiles_per_seq):
    i = pl.program_id(0)
    tm, D = x_ref.shape
    n_chunks = tm // CONV_ROWS
    n_slices = wg_ref.shape[1] // FFN_SLICE

    @pl.when(i % tiles_per_seq == 0)
    def _():
        hist_ref[:HALO_ROWS, :] = jnp.zeros((HALO_ROWS, D), F32)

    @pl.when(i == 0)
    def _():
        mid_ref[...] = jnp.zeros(mid_ref.shape, F32)

    xm = mid_ref[...]
    xmb = xm.astype(BF16)

    x = x_ref[...]
    xb = x.astype(BF16)
    ha = jnp.dot(xb, w1_ref[:, :D], preferred_element_type=F32) + b1_ref[:, :D]
    hg = jnp.dot(xb, w1_ref[:, D:], preferred_element_type=F32) + b1_ref[:, D:]
    hist_ref[HALO_ROWS:, :] = ha * _sigmoid(hg)

    base = HALO_ROWS - (CONV_WIDTH - 1)
    win = CONV_ROWS + HALO_ROWS
    f = jnp.zeros((tm, D), F32)
    acc0 = jnp.zeros((CONV_ROWS, LANES), F32)
    for ci in range(n_chunks):
        row0 = ci * CONV_ROWS
        acc_init = acc0
        for s in range(ci * n_slices // n_chunks, (ci + 1) * n_slices // n_chunks):
            fs = _ffn_slice(xmb, wg_ref, wu_ref, wd_ref, s)
            f = f + fs
            acc0 = jnp.concatenate([_exact_zero_from(fs[:SUBLANES, :LANES])] * (CONV_ROWS // SUBLANES),
                                   axis=0)
        for c in range(D // LANES):
            cols = slice(c * LANES, (c + 1) * LANES)
            a = hist_ref[row0:row0 + win, cols]
            acc = acc_init
            for r in range(SUBLANES):
                rows = CONV_ROWS if r == 0 else CONV_ROWS + SUBLANES
                part = None
                for o in range(r, base + CONV_WIDTH, SUBLANES):
                    j = o - base
                    if j >= 0:
                        a3 = a[o - r:o - r + rows, :].reshape(rows // SUBLANES, SUBLANES, LANES)
                        term = (a3 * wdw_ref[j, :, cols]).reshape(rows, LANES)
                        part = term if part is None else part + term
                if r > 0:
                    part = pltpu.roll(part, rows - r, 0)[:CONV_ROWS, :]
                acc = acc + part
            conv_ref[row0:row0 + CONV_ROWS, cols] = acc + bdw_ref[:, cols]

    hist_ref[:HALO_ROWS, :] = hist_ref[tm:tm + HALO_ROWS, :]
    c = _layer_norm(conv_ref[...], cg_ref[...], cb_ref[...])
    c = c * _sigmoid(c)
    y = jnp.dot(c.astype(BF16), w2_ref[...], preferred_element_type=F32) + b2_ref[...]
    mid_ref[...] = _layer_norm(alpha * x + y, g1_ref[...], bb1_ref[...])
    o_ref[...] = _layer_norm(alpha * xm + f, g2_ref[...], bb2_ref[...])


def _conv_ffn(x, w1, b1, wdw, bdw, cg, cb, w2, b2, g1, bb1, wg, wu, wd, g2, bb2, alpha):
    B, S, D = x.shape
    tm = min(ROW_TILE, S)
    assert S % tm == 0 and tm % CONV_ROWS == 0 and HALO_ROWS >= CONV_WIDTH - 1
    assert wg.shape[1] % FFN_SLICE == 0
    n_tiles = B * S // tm
    wdw_rep = jnp.broadcast_to(wdw[:, None, :], (CONV_WIDTH, SUBLANES, D))
    vec = lambda n: pl.BlockSpec((1, n), lambda i: (0, 0))
    out = pl.pallas_call(
        functools.partial(_conv_ffn_kernel, alpha=alpha, tiles_per_seq=S // tm),
        grid=(n_tiles + 1,),
        in_specs=[pl.BlockSpec((tm, D), lambda i: (jnp.minimum(i, n_tiles - 1), 0)),
                  _resident(w1.shape), vec(2 * D), _resident(wdw_rep.shape), vec(D), vec(D), vec(D),
                  _resident(w2.shape), vec(D), vec(D), vec(D),
                  _resident(wg.shape), _resident(wu.shape), _resident(wd.shape), vec(D), vec(D)],
        out_specs=pl.BlockSpec((tm, D), lambda i: (jnp.maximum(i - 1, 0), 0)),
        out_shape=jax.ShapeDtypeStruct((B * S, D), F32),
        scratch_shapes=[pltpu.VMEM((tm + HALO_ROWS, D), F32), pltpu.VMEM((tm, D), F32),
                        pltpu.VMEM((tm, D), F32)],
        compiler_params=_params(1),
        name="conv_ffn",
    )(x.reshape(B * S, D), w1.astype(BF16), b1.reshape(1, 2 * D), wdw_rep, bdw.reshape(1, D),
      cg.reshape(1, D), cb.reshape(1, D), w2.astype(BF16), b2.reshape(1, D),
      g1.reshape(1, D), bb1.reshape(1, D), wg.astype(BF16), wu.astype(BF16), wd.astype(BF16),
      g2.reshape(1, D), bb2.reshape(1, D))
    return out.reshape(B, S, D)


def kernel(x, attn_w_qkv, attn_w_o, attn_lambda_q1, attn_lambda_k1, attn_lambda_q2, attn_lambda_k2,
           attn_subln_g, conv_w_pw1, conv_b_pw1, conv_w_dw, conv_b_dw, conv_ln_g, conv_ln_b,
           conv_w_pw2, conv_b_pw2, ffn_w_gate, ffn_w_up, ffn_w_down, ln_g, ln_b):
    B, S, D = x.shape
    depth = ln_g.shape[0]
    alpha = (2.0 * depth) ** 0.25
    for i in range(depth):
        j = i // N_MIXERS
        if i % N_MIXERS == 0:
            lambda_init = 0.8 - 0.6 * math.exp(-0.3 * i)
            qt, k, vt = _qkv_rope(x, attn_w_qkv[j])
            lam_vecs = jnp.stack([attn_lambda_q1[j], attn_lambda_k1[j],
                                  attn_lambda_q2[j], attn_lambda_k2[j]]).astype(F32)
            a = _diff_attn(qt, k, vt, lam_vecs, attn_subln_g[j].astype(F32), lambda_init)
            x = _proj_ffn(a.reshape(B * S, D), attn_w_o[j], x.reshape(B * S, D),
                          ln_g[i, 0], ln_b[i, 0], ffn_w_gate[i], ffn_w_up[i], ffn_w_down[i],
                          ln_g[i, 1], ln_b[i, 1], alpha).reshape(B, S, D)
        else:
            x = _conv_ffn(x, conv_w_pw1[j], conv_b_pw1[j], conv_w_dw[j], conv_b_dw[j],
                          conv_ln_g[j], conv_ln_b[j], conv_w_pw2[j], conv_b_pw2[j],
                          ln_g[i, 0], ln_b[i, 0], ffn_w_gate[i], ffn_w_up[i], ffn_w_down[i],
                          ln_g[i, 1], ln_b[i, 1], alpha)
    return x
```
